```python
import jax, jax.numpy as jnp
from jax import lax
import numpy as np

D_MODEL = 2048
BATCH = 8
SEQ = 2048
DEPTH = 2

BRANCH_WIDTH = D_MODEL // 2
N_BRANCHES = 3
SSD_HEAD_DIM = 64
SSD_HEADS = BRANCH_WIDTH // SSD_HEAD_DIM
SSD_GROUPS = 2
SSD_STATE = 128
SSD_CONV = 4
SSD_CHUNK = 128
SSD_XBC = BRANCH_WIDTH + 2 * SSD_GROUPS * SSD_STATE
HGRN_EXPAND = 128
HGRN_HEADS = BRANCH_WIDTH // HGRN_EXPAND
HGRN_VDIM = BRANCH_WIDTH // HGRN_HEADS
HGRN_CHUNK = 64
FOX_HEAD_DIM = 64
FOX_HEADS = BRANCH_WIDTH // FOX_HEAD_DIM
FOX_BLOCK = 128
D_FF = 5504
FFN_CONV = 3
NORM_EPS = 1e-6

IN_SIZES = (
    BRANCH_WIDTH,
    SSD_XBC,
    SSD_HEADS,
    BRANCH_WIDTH,
    BRANCH_WIDTH,
    BRANCH_WIDTH,
    BRANCH_WIDTH,
    BRANCH_WIDTH,
    BRANCH_WIDTH,
    BRANCH_WIDTH,
    FOX_HEADS,
    N_BRANCHES * D_MODEL,
)
D_IN = (BRANCH_WIDTH + SSD_XBC + SSD_HEADS + 4 * BRANCH_WIDTH + 3 * BRANCH_WIDTH
        + FOX_HEADS + N_BRANCHES * D_MODEL)

kernel_name = "hybrid_ssd_hgrn2_fox_gated_block"

F32 = jnp.float32


def split_cols(a, sizes):
    idx = np.cumsum(np.array(sizes))[:-1].tolist()
    return jnp.split(a, idx, axis=-1)


def rms_norm(x, w):
    xf = x.astype(F32)
    y = xf * lax.rsqrt(jnp.mean(xf * xf, axis=-1, keepdims=True) + NORM_EPS)
    return (y * w.astype(F32)).astype(x.dtype)


def grouped_rms_norm(x, w, n_groups):
    shp = x.shape
    xf = x.astype(F32).reshape(shp[:-1] + (n_groups, shp[-1] // n_groups))
    y = xf * lax.rsqrt(jnp.mean(xf * xf, axis=-1, keepdims=True) + NORM_EPS)
    return (y.reshape(shp) * w.astype(F32)).astype(x.dtype)


def causal_depthwise_conv(x, w, b):
    k = w.shape[0]
    y = lax.conv_general_dilated(
        x, w[:, None, :].astype(x.dtype), window_strides=(1,), padding=[(k - 1, 0)],
        dimension_numbers=('NWC', 'WIO', 'NWC'), feature_group_count=x.shape[-1])
    return y + b.astype(x.dtype)


def segsum(a):
    t = a.shape[-1]
    ae = jnp.broadcast_to(a[..., :, None], a.shape + (t,))
    strict = jnp.tril(jnp.ones((t, t), bool), -1)
    cs = jnp.cumsum(jnp.where(strict, ae, 0.0), axis=-2)
    return jnp.where(jnp.tril(jnp.ones((t, t), bool)), cs, -jnp.inf)


def ssd_chunked(x, dt, a, bmat, cmat):
    bsz, t, h, p = x.shape
    g = bmat.shape[2]
    j = h // g
    l = SSD_CHUNK
    c = t // l
    xdt = (x * dt[..., None]).reshape(bsz, c, l, g, j, p)
    adt = (dt * a).reshape(bsz, c, l, g, j).transpose(0, 3, 4, 1, 2)
    bc = bmat.reshape(bsz, c, l, g, -1)
    cc = cmat.reshape(bsz, c, l, g, -1)
    a_cs = jnp.cumsum(adt, axis=-1)
    decay_in = jnp.exp(segsum(adt))
    cb = jnp.einsum('bclgn,bcsgn->bgcls', cc, bc)
    y_diag = jnp.einsum('bgjcls,bcsgjp->bclgjp', cb[:, :, None] * decay_in, xdt)
    decay_st = jnp.exp(a_cs[..., -1:] - a_cs).transpose(0, 3, 4, 1, 2)
    states = jnp.einsum('bclgn,bclgjp->bcgjpn', bc, xdt * decay_st[..., None])
    states = jnp.concatenate([jnp.zeros_like(states[:, :1]), states], axis=1)
    chunk_tot = jnp.pad(a_cs[..., -1], ((0, 0), (0, 0), (0, 0), (1, 0)))
    decay_chunk = jnp.exp(segsum(chunk_tot))
    states = jnp.einsum('bgjzc,bcgjpn->bzgjpn', decay_chunk, states)[:, :-1]
    out_decay = jnp.exp(a_cs).transpose(0, 3, 4, 1, 2)
    y_off = jnp.einsum('bclgn,bcgjpn->bclgjp', cc, states) * out_decay[..., None]
    return (y_diag + y_off).reshape(bsz, t, h, p)


def mamba2_branch(z, xbc, dt_raw, conv_w, conv_b, dt_bias, a_log, d_skip, norm_w):
    bsz, t, _ = z.shape
    xbc = jax.nn.silu(causal_depthwise_conv(xbc, conv_w, conv_b))
    xs, bm, cm = split_cols(xbc, (BRANCH_WIDTH, SSD_GROUPS * SSD_STATE, SSD_GROUPS * SSD_STATE))
    xs = xs.astype(F32).reshape(bsz, t, SSD_HEADS, SSD_HEAD_DIM)
    bm = bm.astype(F32).reshape(bsz, t, SSD_GROUPS, SSD_STATE)
    cm = cm.astype(F32).reshape(bsz, t, SSD_GROUPS, SSD_STATE)
    dt = jax.nn.softplus(dt_raw.astype(F32) + dt_bias.astype(F32))
    a = -jnp.exp(a_log.astype(F32))
    y = ssd_chunked(xs, dt, a, bm, cm) + xs * d_skip.astype(F32)[:, None]
    y = y.reshape(bsz, t, BRANCH_WIDTH).astype(z.dtype)
    return grouped_rms_norm(y * jax.nn.silu(z), norm_w, SSD_GROUPS)


def chunked_gated_recurrence(q, k, v, log_f):
    bsz, t, h, dk = q.shape
    dv = v.shape[-1]
    c = HGRN_CHUNK
    n = t // c

    def to_chunks(u):
        return u.reshape(bsz, n, c, h, u.shape[-1]).transpose(1, 0, 3, 2, 4)

    causal = jnp.tril(jnp.ones((c, c), bool))[:, :, None]

    def step(state, inp):
        qc, kc, vc, gc = inp
        b = jnp.cumsum(gc, axis=2)
        diff = b[:, :, :, None, :] - b[:, :, None, :, :]
        decay = jnp.exp(jnp.where(causal, diff, -jnp.inf))
        scores = jnp.einsum('bhtsk,bhsk->bhts', decay * qc[:, :, :, None, :], kc)
        o = (jnp.einsum('bhts,bhsv->bhtv', scores, vc)
             + jnp.einsum('bhtk,bhkv->bhtv', qc * jnp.exp(b), state))
        b_last = b[:, :, -1]
        new_state = (jnp.exp(b_last)[..., None] * state
                     + jnp.einsum('bhsk,bhsv->bhkv', kc * jnp.exp(b_last[:, :, None] - b), vc))
        return new_state, o

    state0 = jnp.zeros((bsz, h, dk, dv), F32)
    _, o = lax.scan(step, state0, (to_chunks(q), to_chunks(k), to_chunks(v), to_chunks(log_f)))
    return o.transpose(1, 0, 3, 2, 4).reshape(bsz, t, h, dv)


def hgrn2_branch(q_raw, f_raw, i_raw, g_raw, lower_bound, norm_w):
    bsz, t, _ = q_raw.shape

    def heads(u, d):
        return u.astype(F32).reshape(bsz, t, HGRN_HEADS, d)

    lb = lower_bound.astype(F32).reshape(HGRN_HEADS, HGRN_EXPAND)
    q = jax.nn.silu(heads(q_raw, HGRN_EXPAND))
    fr = heads(f_raw, HGRN_EXPAND)
    log_f = jnp.logaddexp(jnp.log(lb), jnp.log1p(-lb) + jax.nn.log_sigmoid(fr))
    k = (1.0 - lb) * jax.nn.sigmoid(-fr)
    v = heads(i_raw, HGRN_VDIM)
    o = chunked_gated_recurrence(q, k, v, log_f)
    o = o.reshape(bsz, t, BRANCH_WIDTH).astype(q_raw.dtype)
    return grouped_rms_norm(o, norm_w, HGRN_HEADS) * jax.nn.silu(g_raw)


def forgetting_attention(q_raw, k_raw, v_raw, f_raw, f_bias):
    bsz, t, _ = q_raw.shape
    nb = t // FOX_BLOCK

    def heads(u):
        return u.reshape(bsz, t, FOX_HEADS, FOX_HEAD_DIM).transpose(0, 2, 1, 3)

    q, k, v = heads(q_raw), heads(k_raw), heads(v_raw)
    log_f = jax.nn.log_sigmoid(f_raw.astype(F32) + f_bias.astype(F32)).transpose(0, 2, 1)
    cum = jnp.cumsum(log_f, axis=-1)
    scale = FOX_HEAD_DIM ** -0.5
    q_blocks = q.reshape(bsz, FOX_HEADS, nb, FOX_BLOCK, FOX_HEAD_DIM).transpose(2, 0, 1, 3, 4)
    c_blocks = cum.reshape(bsz, FOX_HEADS, nb, FOX_BLOCK).transpose(2, 0, 1, 3)
    k_pos = jnp.arange(t)

    def block(args):
        qb, cb, i = args
        s = (jnp.einsum('bhqd,bhkd->bhqk', qb, k).astype(F32) * scale
             + cb[..., None] - cum[:, :, None, :])
        q_pos = i * FOX_BLOCK + jnp.arange(FOX_BLOCK)
        s = jnp.where(q_pos[:, None] >= k_pos[None, :], s, -jnp.inf)
        p = jax.nn.softmax(s, axis=-1)
        return jnp.einsum('bhqk,bhkd->bhqd', p.astype(v.dtype), v)

    o = lax.map(block, (q_blocks, c_blocks, jnp.arange(nb)))
    return o.transpose(1, 0, 3, 2, 4).reshape(bsz, t, BRANCH_WIDTH)


def conv_glu_ffn(h, w_up, conv_w, conv_b, w_down):
    u = causal_depthwise_conv(h @ w_up, conv_w, conv_b)
    gate, up = jnp.split(u, 2, axis=-1)
    return (jax.nn.silu(gate) * up) @ w_down


def setup_inputs(seed: int = 0) -> dict:
    key = jax.random.key(seed)
    ks = jax.random.split(key, 24)
    nrm = lambda k, shape, s: jax.random.normal(k, shape, F32) * s
    dt0 = jnp.exp(jax.random.uniform(ks[6], (DEPTH, SSD_HEADS), F32, np.log(1e-3), np.log(1e-1)))
    return {
        "x": jax.random.normal(ks[0], (BATCH, SEQ, D_MODEL), F32),
        "norm_mix_w": 1.0 + nrm(ks[1], (DEPTH, D_MODEL), 0.02),
        "w_in": nrm(ks[2], (DEPTH, D_MODEL, D_IN), D_MODEL ** -0.5),
        "ssd_conv_w": nrm(ks[3], (DEPTH, SSD_CONV, SSD_XBC), SSD_CONV ** -0.5),
        "ssd_conv_b": nrm(ks[4], (DEPTH, SSD_XBC), 0.02),
        "ssd_dt_bias": dt0 + jnp.log(-jnp.expm1(-dt0)),
        "ssd_a_log": jnp.log(jax.random.uniform(ks[7], (DEPTH, SSD_HEADS), F32, 1.0, 16.0)),
        "ssd_d": 1.0 + nrm(ks[8], (DEPTH, SSD_HEADS), 0.01),
        "ssd_norm_w": 1.0 + nrm(ks[9], (DEPTH, BRANCH_WIDTH), 0.02),
        "hgrn_lb": nrm(ks[10], (DEPTH, BRANCH_WIDTH), 0.1),
        "hgrn_norm_w": 1.0 + nrm(ks[11], (DEPTH, BRANCH_WIDTH), 0.02),
        "fox_f_bias": nrm(ks[12], (DEPTH, FOX_HEADS), 0.1),
        "w_branch_ssd": nrm(ks[13], (DEPTH, BRANCH_WIDTH, D_MODEL), BRANCH_WIDTH ** -0.5),
        "w_branch_hgrn": nrm(ks[14], (DEPTH, BRANCH_WIDTH, D_MODEL), BRANCH_WIDTH ** -0.5),
        "w_branch_fox": nrm(ks[15], (DEPTH, BRANCH_WIDTH, D_MODEL), BRANCH_WIDTH ** -0.5),
        "w_out": nrm(ks[16], (DEPTH, D_MODEL, D_MODEL), D_MODEL ** -0.5),
        "norm_ffn_w": 1.0 + nrm(ks[17], (DEPTH, D_MODEL), 0.02),
        "ffn_w_up": nrm(ks[18], (DEPTH, D_MODEL, 2 * D_FF), D_MODEL ** -0.5),
        "ffn_conv_w": nrm(ks[19], (DEPTH, FFN_CONV, 2 * D_FF), FFN_CONV ** -0.5),
        "ffn_conv_b": nrm(ks[20], (DEPTH, 2 * D_FF), 0.02),
        "ffn_w_down": nrm(ks[21], (DEPTH, D_FF, D_MODEL), D_FF ** -0.5),
        "final_norm_w": 1.0 + nrm(ks[22], (D_MODEL,), 0.02),
    }


def reference(x, norm_mix_w, w_in, ssd_conv_w, ssd_conv_b, ssd_dt_bias, ssd_a_log, ssd_d,
              ssd_norm_w, hgrn_lb, hgrn_norm_w, fox_f_bias, w_branch_ssd, w_branch_hgrn,
              w_branch_fox, w_out, norm_ffn_w, ffn_w_up, ffn_conv_w, ffn_conv_b, ffn_w_down,
              final_norm_w):
    bsz, t, d = x.shape
    lbs = jnp.cumsum(jax.nn.softmax(hgrn_lb.astype(F32), axis=0), axis=0)
    lbs = lbs - lbs[0]
    for l in range(DEPTH):
        h = rms_norm(x, norm_mix_w[l])
        proj = h @ w_in[l]
        (z, xbc, dt_raw, hq, hf, hi, hg, fq, fk, fv, ff, gates) = split_cols(proj, IN_SIZES)
        y_ssd = mamba2_branch(z, xbc, dt_raw, ssd_conv_w[l], ssd_conv_b[l], ssd_dt_bias[l],
                              ssd_a_log[l], ssd_d[l], ssd_norm_w[l])
        y_hgrn = hgrn2_branch(hq, hf, hi, hg, lbs[l], hgrn_norm_w[l])
        y_fox = forgetting_attention(fq, fk, fv, ff, fox_f_bias[l])
        g = jax.nn.sigmoid(gates.astype(F32)).astype(x.dtype).reshape(bsz, t, N_BRANCHES, d)
        merged = (g[:, :, 0] * (y_ssd @ w_branch_ssd[l])
                  + g[:, :, 1] * (y_hgrn @ w_branch_hgrn[l])
                  + g[:, :, 2] * (y_fox @ w_branch_fox[l]))
        x = x + merged @ w_out[l]
        h = rms_norm(x, norm_ffn_w[l])
        x = x + conv_glu_ffn(h, ffn_w_up[l], ffn_conv_w[l], ffn_conv_b[l], ffn_w_down[l])
    return rms_norm(x, final_norm_w)
```

```python
import functools

import jax
import jax.numpy as jnp
from jax import lax
from jax.experimental import pallas as pl
from jax.experimental.pallas import tpu as pltpu

F32 = jnp.float32
BF16 = jnp.bfloat16
HIGHEST = lax.Precision.HIGHEST

D_MODEL = 2048
BRANCH = 1024
SSD_HEADS = 16
SSD_HEAD_DIM = 64
SSD_GROUPS = 2
SSD_STATE = 128
SSD_CONV = 4
SSD_CHUNK = 128
SSD_XBC = BRANCH + 2 * SSD_GROUPS * SSD_STATE
HGRN_HEADS = 8
HGRN_DK = 128
HGRN_CHUNK = 64
HGRN_SUB = 16
FOX_HEADS = 16
FOX_HEAD_DIM = 64
D_FF = 5504
FFN_CONV = 3
NORM_EPS = 1e-6
IN_SIZES = (BRANCH, SSD_XBC, SSD_HEADS, BRANCH, BRANCH, BRANCH, BRANCH, BRANCH, BRANCH, BRANCH,
            FOX_HEADS, 3 * D_MODEL)

LANES = 128
FF_TILE = 512
D_FF_PAD = 5632
FOX_LANE0 = 64
VMEM_LIMIT = 56 * 1024 * 1024


def _fox_lane(h):
    return FOX_LANE0 + 8 * (h // 2) + (h % 2)


def _params(*sem):
    return pltpu.CompilerParams(dimension_semantics=sem, vmem_limit_bytes=VMEM_LIMIT)


def _silu(x):
    return x * jax.nn.sigmoid(x)


def _log_sigmoid(x):
    return jnp.minimum(x, 0.0) - jnp.log1p(jnp.exp(-jnp.abs(x)))


def _dot(a, b, **kw):
    return jnp.dot(a, b, preferred_element_type=F32, **kw)


def _dot_nt(a, b):
    return lax.dot_general(a, b, (((1,), (1,)), ((), ())), preferred_element_type=F32)


def _dot_tn(a, b):
    return lax.dot_general(a, b, (((0,), (0,)), ((), ())), preferred_element_type=F32)


def _tril(n):
    r = lax.broadcasted_iota(jnp.int32, (n, n), 0)
    c = lax.broadcasted_iota(jnp.int32, (n, n), 1)
    return (r >= c).astype(F32)


def _rmsnorm_body(x_ref, w_ref, o_ref):
    x = x_ref[...]
    ms = jnp.mean(x * x, axis=-1, keepdims=True)
    o_ref[...] = (x * lax.rsqrt(ms + NORM_EPS) * w_ref[...]).astype(o_ref.dtype)


def _rmsnorm(x, w, out_dtype):
    m, d = x.shape
    tm = min(512, m)
    return pl.pallas_call(
        _rmsnorm_body,
        grid=(m // tm,),
        in_specs=[pl.BlockSpec((tm, d), lambda i: (i, 0)), pl.BlockSpec((1, d), lambda i: (0, 0))],
        out_specs=pl.BlockSpec((tm, d), lambda i: (i, 0)),
        out_shape=jax.ShapeDtypeStruct((m, d), out_dtype),
        compiler_params=_params("parallel"),
        name="rmsnorm",
    )(x, w.reshape(1, d))


def _mm_body(a_ref, b_ref, o_ref):
    o_ref[...] = _dot(a_ref[...], b_ref[...]).astype(o_ref.dtype)


def _mm_res_body(a_ref, b_ref, r_ref, o_ref):
    o_ref[...] = r_ref[...] + _dot(a_ref[...], b_ref[...])


def _mm(a, b, out_dtype, tn, name, residual=None):
    m, k = a.shape
    n = b.shape[1]
    tm = min(1024, m)
    in_specs = [pl.BlockSpec((tm, k), lambda i, j: (i, 0)), pl.BlockSpec((k, tn), lambda i, j: (0, j))]
    args = [a, b]
    body = _mm_body
    if residual is not None:
        in_specs.append(pl.BlockSpec((tm, tn), lambda i, j: (i, j)))
        args.append(residual)
        body = _mm_res_body
    return pl.pallas_call(
        body,
        grid=(m // tm, n // tn),
        in_specs=in_specs,
        out_specs=pl.BlockSpec((tm, tn), lambda i, j: (i, j)),
        out_shape=jax.ShapeDtypeStruct((m, n), out_dtype),
        compiler_params=_params("parallel", "parallel"),
        name=name,
    )(*args)


def _ssd_body(z_ref, xbc_ref, sm_ref, cw_ref, cb_ref, dtb_ref, alog_ref, d_ref, nw_ref, o_ref,
              halo_ref, s_ref, *, tt):
    @pl.when(pl.program_id(1) == 0)
    def _():
        halo_ref[...] = jnp.zeros_like(halo_ref)
        s_ref[...] = jnp.zeros_like(s_ref)

    xbc = xbc_ref[...]
    ext = jnp.concatenate([halo_ref[...], xbc], axis=0)
    halo_ref[...] = xbc[tt - 8:, :]
    cw = cw_ref[...]
    conv = cb_ref[...] + cw[SSD_CONV - 1:SSD_CONV] * xbc
    for k in range(SSD_CONV - 1):
        back = SSD_CONV - 1 - k
        conv = conv + cw[k:k + 1] * ext[8 - back:8 - back + tt]
    act = _silu(conv)
    xs_all = act[:, :BRANCH]
    dt_all = jax.nn.softplus(sm_ref[...] + dtb_ref[...])
    adt_all = dt_all * (-jnp.exp(alog_ref[...]))

    ln = SSD_CHUNK
    tril = _tril(ln)
    row = lax.broadcasted_iota(jnp.int32, (ln, ln), 0)
    col = lax.broadcasted_iota(jnp.int32, (ln, ln), 1)
    causal = row >= col
    lane = lax.broadcasted_iota(jnp.int32, (ln, LANES), 1)
    eh = lax.broadcasted_iota(jnp.int32, (LANES, BRANCH), 0)
    ec = lax.broadcasted_iota(jnp.int32, (LANES, BRANCH), 1)
    expand = (ec // SSD_HEAD_DIM == eh).astype(F32)
    gw = BRANCH // SSD_GROUPS

    for c in range(tt // ln):
        rs = slice(c * ln, (c + 1) * ln)
        xs = xs_all[rs]
        cs = _dot(tril, adt_all[rs], precision=HIGHEST)
        cs_t = cs.T
        cs_x = _dot(cs, expand, precision=HIGHEST)
        dt_x = _dot(dt_all[rs], expand, precision=HIGHEST)
        xdt = xs * dt_x
        last_x = cs_x[ln - 1:ln]
        xdt_st = (xdt * jnp.exp(last_x - cs_x)).astype(BF16)
        out_decay = jnp.exp(cs_x)
        chunk_decay = jnp.exp(last_x)
        xdt_b = xdt.astype(BF16)
        ys = []
        for g in range(SSD_GROUPS):
            b_g = act[rs, BRANCH + g * SSD_STATE:BRANCH + (g + 1) * SSD_STATE].astype(BF16)
            c_g = act[rs, BRANCH + (SSD_GROUPS + g) * SSD_STATE:
                      BRANCH + (SSD_GROUPS + g + 1) * SSD_STATE].astype(BF16)
            gs = slice(g * gw, (g + 1) * gw)
            cb = _dot_nt(c_g, b_g)
            state = s_ref[g]
            y_off = _dot(c_g, state.astype(BF16)) * out_decay[:, gs]
            s_ref[g] = state * chunk_decay[:, gs] + _dot_tn(b_g, xdt_st[:, gs])
            heads_per_group = SSD_HEADS // SSD_GROUPS
            pieces = []
            for p in range(heads_per_group // 2):
                h0 = g * heads_per_group + 2 * p
                xp = xdt_b[:, h0 * SSD_HEAD_DIM:(h0 + 2) * SSD_HEAD_DIM]
                res = []
                for h in (h0, h0 + 1):
                    seg = jnp.where(causal, cs[:, h:h + 1] - cs_t[h:h + 1, :], -jnp.inf)
                    res.append(_dot((cb * jnp.exp(seg)).astype(BF16), xp))
                pieces.append(jnp.where(lane < SSD_HEAD_DIM, res[0], res[1]))
            ys.append(jnp.concatenate(pieces, axis=1) + y_off)
        y = jnp.concatenate(ys, axis=1) + xs * d_ref[...]
        y = y * _silu(z_ref[rs, :])
        normed = []
        for g in range(SSD_GROUPS):
            yg = y[:, g * gw:(g + 1) * gw]
            ms = jnp.mean(yg * yg, axis=-1, keepdims=True)
            normed.append(yg * lax.rsqrt(ms + NORM_EPS))
        o_ref[rs, :] = (jnp.concatenate(normed, axis=1) * nw_ref[...]).astype(o_ref.dtype)


def _ssd(z, xbc, small, conv_w, conv_b, dt_bias, a_log, d_skip, norm_w, bsz, t):
    tt = min(512, t)
    nt = t // tt
    row = lambda b, i: (b * nt + i, 0)
    fixed = lambda b, i: (0, 0)
    pad = lambda v: jnp.zeros((1, LANES), F32).at[0, :SSD_HEADS].set(v)
    return pl.pallas_call(
        functools.partial(_ssd_body, tt=tt),
        grid=(bsz, nt),
        in_specs=[
            pl.BlockSpec((tt, BRANCH), row),
            pl.BlockSpec((tt, SSD_XBC), row),
            pl.BlockSpec((tt, LANES), row),
            pl.BlockSpec((SSD_CONV, SSD_XBC), fixed),
            pl.BlockSpec((1, SSD_XBC), fixed),
            pl.BlockSpec((1, LANES), fixed),
            pl.BlockSpec((1, LANES), fixed),
            pl.BlockSpec((1, BRANCH), fixed),
            pl.BlockSpec((1, BRANCH), fixed),
        ],
        out_specs=pl.BlockSpec((tt, BRANCH), row),
        out_shape=jax.ShapeDtypeStruct((bsz * t, BRANCH), BF16),
        scratch_shapes=[pltpu.VMEM((8, SSD_XBC), F32),
                        pltpu.VMEM((SSD_GROUPS, SSD_STATE, BRANCH // SSD_GROUPS), F32)],
        compiler_params=_params("parallel", "arbitrary"),
        name="ssd",
    )(z, xbc, small, conv_w, conv_b.reshape(1, -1), pad(dt_bias), pad(a_log),
      jnp.repeat(d_skip, SSD_HEAD_DIM).reshape(1, -1), norm_w.reshape(1, -1))


def _hgrn_body(x_ref, loglb_ref, log1m_ref, onem_ref, nw_ref, o_ref, st_ref, *, tt):
    @pl.when(pl.program_id(1) == 0)
    def _():
        st_ref[...] = jnp.zeros_like(st_ref)

    cl, sub = HGRN_CHUNK, HGRN_SUB
    nsub = cl // sub
    tril = _tril(cl)
    t_idx = lax.broadcasted_iota(jnp.int32, (sub, HGRN_DK), 0)
    lane = lax.broadcasted_iota(jnp.int32, (sub, cl), 1)

    def chunk(c, carry):
        r0 = pl.multiple_of(c * cl, cl)
        hq = x_ref[pl.ds(r0, cl), 0 * BRANCH:1 * BRANCH]
        hf = x_ref[pl.ds(r0, cl), 1 * BRANCH:2 * BRANCH]
        hi = x_ref[pl.ds(r0, cl), 2 * BRANCH:3 * BRANCH]
        hg = x_ref[pl.ds(r0, cl), 3 * BRANCH:4 * BRANCH]
        la = loglb_ref[...]
        lb2 = log1m_ref[...] + _log_sigmoid(hf)
        log_f = jnp.maximum(la, lb2) + jnp.log1p(jnp.exp(-jnp.abs(la - lb2)))
        kk = onem_ref[...] * jax.nn.sigmoid(-hf)
        q = _silu(hq)
        b = _dot(tril, log_f, precision=HIGHEST)
        b_last = b[cl - 1:cl]
        q_in = (q * jnp.exp(b)).astype(BF16)
        k_st = (kk * jnp.exp(b_last - b)).astype(BF16)
        dec = jnp.exp(b_last)
        v_b = hi.astype(BF16)
        outs = []
        for h in range(HGRN_HEADS):
            sl = slice(h * HGRN_DK, (h + 1) * HGRN_DK)
            bh, qh, kh, vh = b[:, sl], q[:, sl], kk[:, sl], v_b[:, sl]
            st = st_ref[h]
            o_h = _dot_nt(q_in[:, sl], st.astype(BF16))
            st_ref[h] = st * dec[:, sl] + _dot_tn(vh, k_st[:, sl])
            a_rows = []
            for i in range(nsub):
                i0 = i * sub
                bi, qi = bh[i0:i0 + sub], qh[i0:i0 + sub]
                lhs, rhs = [], []
                if i > 0:
                    beta = bh[i0 - 1:i0]
                    lhs.append(qi * jnp.exp(bi - beta))
                    rhs.append(kh[:i0] * jnp.exp(beta - bh[:i0]))
                for s in range(sub):
                    lhs.append(qi * jnp.exp(jnp.where(t_idx >= s, bi - bi[s:s + 1], -jnp.inf)))
                rhs.append(kh[i0:i0 + sub])
                if i0 + sub < cl:
                    rhs.append(jnp.zeros((cl - i0 - sub, HGRN_DK), F32))
                res = _dot_nt(jnp.concatenate(lhs, axis=0).astype(BF16),
                              jnp.concatenate(rhs, axis=0).astype(BF16))
                off = sub if i > 0 else 0
                a_i = jnp.where(lane < i0, res[:sub], 0.0) if i > 0 else jnp.zeros((sub, cl), F32)
                for s in range(sub):
                    a_i = a_i + jnp.where(lane == i0 + s, res[off + s * sub:off + (s + 1) * sub], 0.0)
                a_rows.append(a_i)
            o_h = o_h + _dot(jnp.concatenate(a_rows, axis=0).astype(BF16), vh)
            ms = jnp.mean(o_h * o_h, axis=-1, keepdims=True)
            outs.append(o_h * lax.rsqrt(ms + NORM_EPS))
        o = jnp.concatenate(outs, axis=1) * nw_ref[...]
        o_ref[pl.ds(r0, cl), :] = (o * _silu(hg)).astype(o_ref.dtype)
        return carry

    lax.fori_loop(0, tt // cl, chunk, 0)


def _hgrn(proj, lb, norm_w, bsz, t):
    tt = min(256, t)
    nt = t // tt
    row = lambda b, i: (b * nt + i, 0)
    fixed = lambda b, i: (0, 0)
    lb = lb.reshape(1, BRANCH)
    return pl.pallas_call(
        functools.partial(_hgrn_body, tt=tt),
        grid=(bsz, nt),
        in_specs=[pl.BlockSpec((tt, 4 * BRANCH), row)] + [pl.BlockSpec((1, BRANCH), fixed)] * 4,
        out_specs=pl.BlockSpec((tt, BRANCH), row),
        out_shape=jax.ShapeDtypeStruct((bsz * t, BRANCH), BF16),
        scratch_shapes=[pltpu.VMEM((HGRN_HEADS, HGRN_DK, HGRN_DK), F32)],
        compiler_params=_params("parallel", "arbitrary"),
        name="hgrn",
    )(proj, jnp.log(lb), jnp.log1p(-lb), 1.0 - lb, norm_w.reshape(1, BRANCH))


def _fox_prep_body(sm_ref, bias_ref, ccol_ref, crow_ref, *, t):
    ln = LANES
    tril = _tril(ln)
    carry = jnp.zeros((1, LANES), F32)
    for c in range(t // ln):
        rs = slice(c * ln, (c + 1) * ln)
        lf = _log_sigmoid(sm_ref[rs, :] + bias_ref[...])
        cs = _dot(tril, lf, precision=HIGHEST) + carry
        carry = cs[ln - 1:ln]
        crow_ref[0, :, rs] = cs.T
        for p in range(FOX_HEADS // 2):
            ccol_ref[rs, p * LANES:(p + 1) * LANES] = pltpu.roll(cs, LANES - _fox_lane(2 * p), 1)


def _fox_prep(small, f_bias, bsz, t):
    bias = jnp.zeros((1, LANES), F32).at[0, jnp.array([_fox_lane(h) for h in range(FOX_HEADS)])].set(f_bias)
    return pl.pallas_call(
        functools.partial(_fox_prep_body, t=t),
        grid=(bsz,),
        in_specs=[pl.BlockSpec((t, LANES), lambda b: (b, 0)), pl.BlockSpec((1, LANES), lambda b: (0, 0))],
        out_specs=[pl.BlockSpec((t, BRANCH), lambda b: (b, 0)),
                   pl.BlockSpec((1, LANES, t), lambda b: (b, 0, 0))],
        out_shape=[jax.ShapeDtypeStruct((bsz * t, BRANCH), F32),
                   jax.ShapeDtypeStruct((bsz, LANES, t), F32)],
        compiler_params=_params("parallel"),
        name="fox_prep",
    )(small, bias)


def _fox_body(q_ref, k_ref, v_ref, ccol_ref, crow_ref, o_ref, *, tq):
    qi = pl.program_id(2)
    lane = lax.broadcasted_iota(jnp.int32, (tq, LANES), 1)
    row = lax.broadcasted_iota(jnp.int32, (tq, tq), 0)
    col = lax.broadcasted_iota(jnp.int32, (tq, tq), 1)
    q = q_ref[...] * (FOX_HEAD_DIM ** -0.5)
    zero = jnp.zeros_like(q)
    qs = (jnp.where(lane < FOX_HEAD_DIM, q, zero), jnp.where(lane >= FOX_HEAD_DIM, q, zero))
    cq = (ccol_ref[:, 0:1], ccol_ref[:, 1:2])

    def step(kj, carry, masked):
        start = pl.multiple_of(kj * tq, tq)
        kb = k_ref[pl.ds(start, tq), :]
        vb = v_ref[pl.ds(start, tq), :]
        cr = crow_ref[0, :, pl.ds(start, tq)]
        new = []
        for h in range(2):
            m, l, acc = carry[h]
            s = _dot_nt(qs[h], kb) + cq[h] - cr[h:h + 1, :]
            if masked:
                s = jnp.where(row >= col, s, -jnp.inf)
            m_new = jnp.maximum(m, jnp.max(s, axis=1, keepdims=True))
            alpha = jnp.exp(m - m_new)
            p = jnp.exp(s - m_new)
            l = alpha * l + jnp.sum(p, axis=1, keepdims=True)
            acc = alpha * acc + _dot(p.astype(BF16), vb)
            new.append((m_new, l, acc))
        return tuple(new)

    init = tuple((jnp.full((tq, 1), -1e30, F32), jnp.zeros((tq, 1), F32), jnp.zeros((tq, LANES), F32))
                 for _ in range(2))
    carry = lax.fori_loop(0, qi, lambda kj, cr: step(kj, cr, False), init)
    (_, l0, a0), (_, l1, a1) = step(qi, carry, True)
    o_ref[...] = jnp.where(lane < FOX_HEAD_DIM, a0 / l0, a1 / l1).astype(o_ref.dtype)


def _fox(qkv, ccol, crow, bsz, t):
    tq = min(256, t)
    nq = t // tq
    pairs = FOX_HEADS // 2
    return pl.pallas_call(
        functools.partial(_fox_body, tq=tq),
        grid=(bsz, pairs, nq),
        in_specs=[
            pl.BlockSpec((tq, LANES), lambda b, p, i: (b * nq + i, p)),
            pl.BlockSpec((t, LANES), lambda b, p, i: (b, pairs + p)),
            pl.BlockSpec((t, LANES), lambda b, p, i: (b, 2 * pairs + p)),
            pl.BlockSpec((tq, LANES), lambda b, p, i: (b * nq + i, p)),
            pl.BlockSpec((1, 8, t), lambda b, p, i: (b, FOX_LANE0 // 8 + p, 0)),
        ],
        out_specs=pl.BlockSpec((tq, LANES), lambda b, p, i: (b * nq + i, p)),
        out_shape=jax.ShapeDtypeStruct((bsz * t, BRANCH), BF16),
        compiler_params=_params("parallel", "parallel", "arbitrary"),
        name="fox",
    )(qkv, qkv, qkv, ccol, crow)


def _merge_body(y0_ref, y1_ref, y2_ref, w0_ref, w1_ref, w2_ref, g0_ref, g1_ref, g2_ref, o_ref):
    acc = jax.nn.sigmoid(g0_ref[...]) * _dot(y0_ref[...], w0_ref[...])
    acc = acc + jax.nn.sigmoid(g1_ref[...]) * _dot(y1_ref[...], w1_ref[...])
    acc = acc + jax.nn.sigmoid(g2_ref[...]) * _dot(y2_ref[...], w2_ref[...])
    o_ref[...] = acc.astype(o_ref.dtype)


def _merge(ys, ws, gates):
    m = gates.shape[0]
    tm = min(512, m)
    tn = 1024
    nj = D_MODEL // tn
    y_spec = pl.BlockSpec((tm, BRANCH), lambda i, j: (i, 0))
    w_spec = pl.BlockSpec((BRANCH, tn), lambda i, j: (0, j))
    g_specs = [pl.BlockSpec((tm, tn), functools.partial(lambda i, j, r: (i, r * nj + j), r=r)) for r in range(3)]
    return pl.pallas_call(
        _merge_body,
        grid=(m // tm, nj),
        in_specs=[y_spec] * 3 + [w_spec] * 3 + g_specs,
        out_specs=pl.BlockSpec((tm, tn), lambda i, j: (i, j)),
        out_shape=jax.ShapeDtypeStruct((m, D_MODEL), BF16),
        compiler_params=_params("parallel", "parallel"),
        name="merge",
    )(*ys, *ws, gates, gates, gates)


def _ffn_body(h_ref, halo_ref, x_ref, wgu_ref, cw_ref, cb_ref, wd_ref, o_ref, hh_ref, acc_ref, *, tm, tiles_per_seq):
    i = pl.program_id(0)
    j = pl.program_id(1)
    pad = 16

    @pl.when(j == 0)
    def _():
        first = (i % tiles_per_seq) == 0
        halo = halo_ref[...]
        hh_ref[:pad, :] = jnp.where(first, jnp.zeros_like(halo), halo)
        hh_ref[pad:, :] = h_ref[...]
        acc_ref[...] = jnp.zeros_like(acc_ref)

    u = _dot(hh_ref[...], wgu_ref[...])
    cw = cw_ref[0]
    conv = cb_ref[0] + cw[FFN_CONV - 1:FFN_CONV] * u[pad:]
    for k in range(FFN_CONV - 1):
        back = FFN_CONV - 1 - k
        conv = conv + cw[k:k + 1] * u[pad - back:pad - back + tm]
    act = (_silu(conv[:, :FF_TILE]) * conv[:, FF_TILE:]).astype(BF16)
    acc_ref[...] += _dot(act, wd_ref[...])

    @pl.when(j == pl.num_programs(1) - 1)
    def _():
        o_ref[...] = x_ref[...] + acc_ref[...]


def _ffn(h, x, wgu, conv_w, conv_b, wd, t):
    m = x.shape[0]
    tm = min(512, t)
    nf = D_FF_PAD // FF_TILE
    hb = tm // 16
    return pl.pallas_call(
        functools.partial(_ffn_body, tm=tm, tiles_per_seq=t // tm),
        grid=(m // tm, nf),
        in_specs=[
            pl.BlockSpec((tm, D_MODEL), lambda i, j: (i, 0)),
            pl.BlockSpec((16, D_MODEL), lambda i, j: (jnp.maximum(i * hb - 1, 0), 0)),
            pl.BlockSpec((tm, D_MODEL), lambda i, j: (i, 0)),
            pl.BlockSpec((D_MODEL, 2 * FF_TILE), lambda i, j: (0, j)),
            pl.BlockSpec((1, FFN_CONV, 2 * FF_TILE), lambda i, j: (j, 0, 0)),
            pl.BlockSpec((1, 1, 2 * FF_TILE), lambda i, j: (j, 0, 0)),
            pl.BlockSpec((FF_TILE, D_MODEL), lambda i, j: (j, 0)),
        ],
        out_specs=pl.BlockSpec((tm, D_MODEL), lambda i, j: (i, 0)),
        out_shape=jax.ShapeDtypeStruct((m, D_MODEL), F32),
        scratch_shapes=[pltpu.VMEM((16 + tm, D_MODEL), BF16), pltpu.VMEM((tm, D_MODEL), F32)],
        compiler_params=_params("parallel", "arbitrary"),
        name="ffn",
    )(h, h, x, wgu, conv_w, conv_b, wd)


def _ffn_weights(w_up, conv_w, conv_b, w_down):
    nf = D_FF_PAD // FF_TILE
    padc = lambda a: jnp.pad(a, [(0, 0)] * (a.ndim - 1) + [(0, D_FF_PAD - D_FF)])
    tile = lambda a: padc(a).reshape(a.shape[:-1] + (nf, FF_TILE))
    pair = lambda a: jnp.concatenate([tile(a[..., :D_FF]), tile(a[..., D_FF:])], axis=-1)
    wgu = pair(w_up).reshape(D_MODEL, 2 * D_FF_PAD).astype(BF16)
    cw = jnp.moveaxis(pair(conv_w), 1, 0)
    cb = pair(conv_b).reshape(nf, 1, 2 * FF_TILE)
    wd = jnp.pad(w_down, ((0, D_FF_PAD - D_FF), (0, 0))).astype(BF16)
    return wgu, cw, cb, wd


def _in_proj_weights(w):
    offs = [0]
    for s in IN_SIZES:
        offs.append(offs[-1] + s)
    seg = lambda a, b: w[:, offs[a]:offs[b]]
    small = jnp.zeros((D_MODEL, LANES), F32).at[:, :SSD_HEADS].set(seg(2, 3))
    small = small.at[:, jnp.array([_fox_lane(h) for h in range(FOX_HEADS)])].set(seg(10, 11))
    cast = lambda a: a.astype(BF16)
    return dict(z=cast(seg(0, 1)), xbc=cast(seg(1, 2)), hgrn=cast(seg(3, 7)), fox=cast(seg(7, 10)),
                gates=cast(seg(11, 12)), small=cast(small))


def kernel(x, norm_mix_w, w_in, ssd_conv_w, ssd_conv_b, ssd_dt_bias, ssd_a_log, ssd_d, ssd_norm_w, hgrn_lb, hgrn_norm_w, fox_f_bias, w_branch_ssd, w_branch_hgrn, w_branch_fox, w_out, norm_ffn_w, ffn_w_up, ffn_conv_w, ffn_conv_b, ffn_w_down, final_norm_w):
    bsz, t, d = x.shape
    depth = w_in.shape[0]
    lbs = jnp.cumsum(jax.nn.softmax(hgrn_lb.astype(F32), axis=0), axis=0)
    lbs = lbs - lbs[0]
    x = x.reshape(bsz * t, d)
    for l in range(depth):
        wi = _in_proj_weights(w_in[l])
        h = _rmsnorm(x, norm_mix_w[l], BF16)
        z = _mm(h, wi["z"], F32, 1024, "proj_z")
        xbc = _mm(h, wi["xbc"], F32, SSD_XBC, "proj_xbc")
        small = _mm(h, wi["small"], F32, LANES, "proj_small")
        p_hgrn = _mm(h, wi["hgrn"], F32, 1024, "proj_hgrn")
        p_fox = _mm(h, wi["fox"], BF16, 1024, "proj_fox")
        gates = _mm(h, wi["gates"], F32, 1024, "proj_gates")
        y_ssd = _ssd(z, xbc, small, ssd_conv_w[l], ssd_conv_b[l], ssd_dt_bias[l], ssd_a_log[l], ssd_d[l],
                     ssd_norm_w[l], bsz, t)
        y_hgrn = _hgrn(p_hgrn, lbs[l], hgrn_norm_w[l], bsz, t)
        ccol, crow = _fox_prep(small, fox_f_bias[l], bsz, t)
        y_fox = _fox(p_fox, ccol, crow, bsz, t)
        merged = _merge((y_ssd, y_hgrn, y_fox),
                        (w_branch_ssd[l].astype(BF16), w_branch_hgrn[l].astype(BF16), w_branch_fox[l].astype(BF16)),
                        gates)
        x = _mm(merged, w_out[l].astype(BF16), F32, 1024, "out_proj", residual=x)
        h = _rmsnorm(x, norm_ffn_w[l], BF16)
        x = _ffn(h, x, *_ffn_weights(ffn_w_up[l], ffn_conv_w[l], ffn_conv_b[l], ffn_w_down[l]), t)
    return _rmsnorm(x, final_norm_w, F32).reshape(bsz, t, d)
```

```python
import functools

import jax
import jax.numpy as jnp
from jax import lax
from jax.experimental import pallas as pl
from jax.experimental.pallas import tpu as pltpu

F32 = jnp.float32
BF16 = jnp.bfloat16
HIGHEST = lax.Precision.HIGHEST

D_MODEL = 2048
BRANCH = 1024
SSD_HEADS = 16
SSD_HEAD_DIM = 64
SSD_GROUPS = 2
SSD_STATE = 128
SSD_CONV = 4
SSD_CHUNK = 128
SSD_XBC = BRANCH + 2 * SSD_GROUPS * SSD_STATE
HGRN_HEADS = 8
HGRN_DK = 128
HGRN_CHUNK = 64
HGRN_SUB = 8
FOX_HEADS = 16
FOX_HEAD_DIM = 64
D_FF = 5504
FFN_CONV = 3
NORM_EPS = 1e-6
IN_SIZES = (BRANCH, SSD_XBC, SSD_HEADS, BRANCH, BRANCH, BRANCH, BRANCH, BRANCH, BRANCH, BRANCH,
            FOX_HEADS, 3 * D_MODEL)

LANES = 128
FF_TILE = 512
D_FF_PAD = 5632
VMEM_LIMIT = 56 * 1024 * 1024

_IN_OFFS = [sum(IN_SIZES[:i]) for i in range(len(IN_SIZES) + 1)]
DT_LANE0 = _IN_OFFS[2] % LANES
FOX_LANE0 = _IN_OFFS[10] % LANES
assert DT_LANE0 == 0 and FOX_LANE0 + FOX_HEADS <= LANES


def _params(*sem):
    return pltpu.CompilerParams(dimension_semantics=sem, vmem_limit_bytes=VMEM_LIMIT)


def _silu(x):
    return x * jax.nn.sigmoid(x)


def _log_sigmoid(x):
    return jnp.minimum(x, 0.0) - jnp.log1p(jnp.exp(-jnp.abs(x)))


def _dot(a, b, **kw):
    return jnp.dot(a, b, preferred_element_type=F32, **kw)


def _dot_nt(a, b):
    return lax.dot_general(a, b, (((1,), (1,)), ((), ())), preferred_element_type=F32)


def _dot_tn(a, b):
    return lax.dot_general(a, b, (((0,), (0,)), ((), ())), preferred_element_type=F32)


def _tril(n):
    r = lax.broadcasted_iota(jnp.int32, (n, n), 0)
    c = lax.broadcasted_iota(jnp.int32, (n, n), 1)
    return (r >= c).astype(F32)


def _rmsnorm_body(x_ref, w_ref, o_ref):
    x = x_ref[...]
    ms = jnp.mean(x * x, axis=-1, keepdims=True)
    o_ref[...] = (x * lax.rsqrt(ms + NORM_EPS) * w_ref[...]).astype(o_ref.dtype)


def _rmsnorm(x, w, out_dtype):
    m, d = x.shape
    tm = min(512, m)
    return pl.pallas_call(
        _rmsnorm_body,
        grid=(m // tm,),
        in_specs=[pl.BlockSpec((tm, d), lambda i: (i, 0)), pl.BlockSpec((1, d), lambda i: (0, 0))],
        out_specs=pl.BlockSpec((tm, d), lambda i: (i, 0)),
        out_shape=jax.ShapeDtypeStruct((m, d), out_dtype),
        compiler_params=_params("parallel"),
        name="rmsnorm",
    )(x, w.reshape(1, d))


def _mm_body(a_ref, b_ref, o_ref):
    o_ref[...] = _dot(a_ref[...], b_ref[...]).astype(o_ref.dtype)


def _mm_nt_body(a_ref, b_ref, o_ref):
    o_ref[...] = _dot_nt(a_ref[...], b_ref[...]).astype(o_ref.dtype)


def _mm_res_body(a_ref, b_ref, r_ref, o_ref):
    o_ref[...] = r_ref[...] + _dot(a_ref[...], b_ref[...])


def _mm(a, b, out_dtype, tn, name, residual=None, b_is_nk=False):
    m, k = a.shape
    n = b.shape[0] if b_is_nk else b.shape[1]
    tm = min(1024, m)
    b_spec = pl.BlockSpec((tn, k), lambda i, j: (j, 0)) if b_is_nk else pl.BlockSpec((k, tn), lambda i, j: (0, j))
    in_specs = [pl.BlockSpec((tm, k), lambda i, j: (i, 0)), b_spec]
    args = [a, b]
    body = _mm_nt_body if b_is_nk else _mm_body
    if residual is not None:
        in_specs.append(pl.BlockSpec((tm, tn), lambda i, j: (i, j)))
        args.append(residual)
        body = _mm_res_body
    return pl.pallas_call(
        body,
        grid=(m // tm, n // tn),
        in_specs=in_specs,
        out_specs=pl.BlockSpec((tm, tn), lambda i, j: (i, j)),
        out_shape=jax.ShapeDtypeStruct((m, n), out_dtype),
        compiler_params=_params("parallel", "parallel"),
        name=name,
    )(*args)


def _ssd_body(z_ref, xbc_ref, sm_ref, cw_ref, cb_ref, dtb_ref, alog_ref, d_ref, nw_ref, o_ref,
              halo_ref, s_ref, *, tt):
    @pl.when(pl.program_id(1) == 0)
    def _():
        halo_ref[...] = jnp.zeros_like(halo_ref)
        s_ref[...] = jnp.zeros_like(s_ref)

    xbc = xbc_ref[...]
    ext = jnp.concatenate([halo_ref[...], xbc], axis=0)
    halo_ref[...] = xbc[tt - 8:, :]
    cw = cw_ref[...]
    conv = cb_ref[...] + cw[SSD_CONV - 1:SSD_CONV] * xbc
    for k in range(SSD_CONV - 1):
        back = SSD_CONV - 1 - k
        conv = conv + cw[k:k + 1] * ext[8 - back:8 - back + tt]
    act = _silu(conv)
    xs_all = act[:, :BRANCH]
    dt_all = jax.nn.softplus(sm_ref[...] + dtb_ref[...])
    adt_all = dt_all * (-jnp.exp(alog_ref[...]))

    ln = SSD_CHUNK
    tril = _tril(ln)
    row = lax.broadcasted_iota(jnp.int32, (ln, ln), 0)
    col = lax.broadcasted_iota(jnp.int32, (ln, ln), 1)
    causal = row >= col
    lane = lax.broadcasted_iota(jnp.int32, (ln, LANES), 1)
    eh = lax.broadcasted_iota(jnp.int32, (LANES, BRANCH), 0)
    ec = lax.broadcasted_iota(jnp.int32, (LANES, BRANCH), 1)
    expand = (ec // SSD_HEAD_DIM == eh).astype(F32)
    gw = BRANCH // SSD_GROUPS

    for c in range(tt // ln):
        rs = slice(c * ln, (c + 1) * ln)
        xs = xs_all[rs]
        cs = _dot(tril, adt_all[rs], precision=HIGHEST)
        cs_t = cs.T
        cs_x = _dot(cs, expand, precision=HIGHEST)
        dt_x = _dot(dt_all[rs], expand, precision=HIGHEST)
        xdt = xs * dt_x
        last_x = cs_x[ln - 1:ln]
        xdt_st = (xdt * jnp.exp(last_x - cs_x)).astype(BF16)
        out_decay = jnp.exp(cs_x)
        chunk_decay = jnp.exp(last_x)
        xdt_b = xdt.astype(BF16)
        ys = []
        for g in range(SSD_GROUPS):
            b_g = act[rs, BRANCH + g * SSD_STATE:BRANCH + (g + 1) * SSD_STATE].astype(BF16)
            c_g = act[rs, BRANCH + (SSD_GROUPS + g) * SSD_STATE:
                      BRANCH + (SSD_GROUPS + g + 1) * SSD_STATE].astype(BF16)
            gs = slice(g * gw, (g + 1) * gw)
            cb = _dot_nt(c_g, b_g)
            state = s_ref[g]
            y_off = _dot(c_g, state.astype(BF16)) * out_decay[:, gs]
            s_ref[g] = state * chunk_decay[:, gs] + _dot_tn(b_g, xdt_st[:, gs])
            heads_per_group = SSD_HEADS // SSD_GROUPS
            pieces = []
            for p in range(heads_per_group // 2):
                h0 = g * heads_per_group + 2 * p
                xp = xdt_b[:, h0 * SSD_HEAD_DIM:(h0 + 2) * SSD_HEAD_DIM]
                res = []
                for h in (h0, h0 + 1):
                    seg = jnp.where(causal, cs[:, h:h + 1] - cs_t[h:h + 1, :], -jnp.inf)
                    res.append(_dot((cb * jnp.exp(seg)).astype(BF16), xp))
                pieces.append(jnp.where(lane < SSD_HEAD_DIM, res[0], res[1]))
            ys.append(jnp.concatenate(pieces, axis=1) + y_off)
        y = jnp.concatenate(ys, axis=1) + xs * d_ref[...]
        y = y * _silu(z_ref[rs, :])
        normed = []
        for g in range(SSD_GROUPS):
            yg = y[:, g * gw:(g + 1) * gw]
            ms = jnp.mean(yg * yg, axis=-1, keepdims=True)
            normed.append(yg * lax.rsqrt(ms + NORM_EPS))
        o_ref[rs, :] = (jnp.concatenate(normed, axis=1) * nw_ref[...]).astype(o_ref.dtype)


def _ssd(z, xbc, small, conv_w, conv_b, dt_bias, a_log, d_skip, norm_w, bsz, t):
    tt = min(512, t)
    nt = t // tt
    row = lambda b, i: (b * nt + i, 0)
    fixed = lambda b, i: (0, 0)
    pad = lambda v: jnp.pad(v, (0, LANES - SSD_HEADS)).reshape(1, LANES)
    return pl.pallas_call(
        functools.partial(_ssd_body, tt=tt),
        grid=(bsz, nt),
        in_specs=[
            pl.BlockSpec((tt, BRANCH), row),
            pl.BlockSpec((tt, SSD_XBC), row),
            pl.BlockSpec((tt, LANES), row),
            pl.BlockSpec((SSD_CONV, SSD_XBC), fixed),
            pl.BlockSpec((1, SSD_XBC), fixed),
            pl.BlockSpec((1, LANES), fixed),
            pl.BlockSpec((1, LANES), fixed),
            pl.BlockSpec((1, BRANCH), fixed),
            pl.BlockSpec((1, BRANCH), fixed),
        ],
        out_specs=pl.BlockSpec((tt, BRANCH), row),
        out_shape=jax.ShapeDtypeStruct((bsz * t, BRANCH), BF16),
        scratch_shapes=[pltpu.VMEM((8, SSD_XBC), F32),
                        pltpu.VMEM((SSD_GROUPS, SSD_STATE, BRANCH // SSD_GROUPS), F32)],
        compiler_params=_params("parallel", "arbitrary"),
        name="ssd",
    )(z, xbc, small, conv_w, conv_b.reshape(1, -1), pad(dt_bias), pad(a_log),
      jnp.repeat(d_skip, SSD_HEAD_DIM).reshape(1, -1), norm_w.reshape(1, -1))


def _hgrn_body(x_ref, loglb_ref, log1m_ref, onem_ref, nw_ref, o_ref, st_ref, *, tt):
    @pl.when(pl.program_id(1) == 0)
    def _():
        st_ref[...] = jnp.zeros_like(st_ref)

    cl, sub = HGRN_CHUNK, HGRN_SUB
    nsub = cl // sub
    tril = _tril(cl)
    t_idx = lax.broadcasted_iota(jnp.int32, (sub, HGRN_DK), 0)
    lane = lax.broadcasted_iota(jnp.int32, (sub, cl), 1)

    def chunk(c, carry):
        r0 = pl.multiple_of(c * cl, cl)
        hq = x_ref[pl.ds(r0, cl), 0 * BRANCH:1 * BRANCH]
        hf = x_ref[pl.ds(r0, cl), 1 * BRANCH:2 * BRANCH]
        hi = x_ref[pl.ds(r0, cl), 2 * BRANCH:3 * BRANCH]
        hg = x_ref[pl.ds(r0, cl), 3 * BRANCH:4 * BRANCH]
        la = loglb_ref[...]
        lb2 = log1m_ref[...] + _log_sigmoid(hf)
        log_f = jnp.maximum(la, lb2) + jnp.log1p(jnp.exp(-jnp.abs(la - lb2)))
        kk = onem_ref[...] * jax.nn.sigmoid(-hf)
        q = _silu(hq)
        b = _dot(tril, log_f, precision=HIGHEST)
        b_last = b[cl - 1:cl]
        q_in = (q * jnp.exp(b)).astype(BF16)
        k_st = (kk * jnp.exp(b_last - b)).astype(BF16)
        dec = jnp.exp(b_last)
        v_b = hi.astype(BF16)
        outs = []
        for h in range(HGRN_HEADS):
            sl = slice(h * HGRN_DK, (h + 1) * HGRN_DK)
            bh, qh, kh, vh = b[:, sl], q[:, sl], kk[:, sl], v_b[:, sl]
            st = st_ref[h]
            o_h = _dot_nt(q_in[:, sl], st.astype(BF16))
            st_ref[h] = st * dec[:, sl] + _dot_tn(vh, k_st[:, sl])
            a_rows = []
            for i in range(nsub):
                i0 = i * sub
                bi, qi = bh[i0:i0 + sub], qh[i0:i0 + sub]
                lhs, rhs = [], []
                if i > 0:
                    beta = bh[i0 - 1:i0]
                    lhs.append(qi * jnp.exp(bi - beta))
                    rhs.append(kh[:i0] * jnp.exp(beta - bh[:i0]))
                for s in range(sub):
                    lhs.append(qi * jnp.exp(jnp.where(t_idx >= s, bi - bi[s:s + 1], -jnp.inf)))
                rhs.append(kh[i0:i0 + sub])
                if i0 + sub < cl:
                    rhs.append(jnp.zeros((cl - i0 - sub, HGRN_DK), F32))
                res = _dot_nt(jnp.concatenate(lhs, axis=0).astype(BF16),
                              jnp.concatenate(rhs, axis=0).astype(BF16))
                off = sub if i > 0 else 0
                a_i = jnp.where(lane < i0, res[:sub], 0.0) if i > 0 else jnp.zeros((sub, cl), F32)
                for s in range(sub):
                    a_i = a_i + jnp.where(lane == i0 + s, res[off + s * sub:off + (s + 1) * sub], 0.0)
                a_rows.append(a_i)
            o_h = o_h + _dot(jnp.concatenate(a_rows, axis=0).astype(BF16), vh)
            ms = jnp.mean(o_h * o_h, axis=-1, keepdims=True)
            outs.append(o_h * lax.rsqrt(ms + NORM_EPS))
        o = jnp.concatenate(outs, axis=1) * nw_ref[...]
        o_ref[pl.ds(r0, cl), :] = (o * _silu(hg)).astype(o_ref.dtype)
        return carry

    lax.fori_loop(0, tt // cl, chunk, 0)


def _hgrn(proj, lb, norm_w, bsz, t):
    tt = min(256, t)
    nt = t // tt
    row = lambda b, i: (b * nt + i, 0)
    fixed = lambda b, i: (0, 0)
    lb = lb.reshape(1, BRANCH)
    return pl.pallas_call(
        functools.partial(_hgrn_body, tt=tt),
        grid=(bsz, nt),
        in_specs=[pl.BlockSpec((tt, 4 * BRANCH), row)] + [pl.BlockSpec((1, BRANCH), fixed)] * 4,
        out_specs=pl.BlockSpec((tt, BRANCH), row),
        out_shape=jax.ShapeDtypeStruct((bsz * t, BRANCH), BF16),
        scratch_shapes=[pltpu.VMEM((HGRN_HEADS, HGRN_DK, HGRN_DK), F32)],
        compiler_params=_params("parallel", "arbitrary"),
        name="hgrn",
    )(proj, jnp.log(lb), jnp.log1p(-lb), 1.0 - lb, norm_w.reshape(1, BRANCH))


def _fox_prep_body(sm_ref, bias_ref, ccol_ref, *, t):
    ln = LANES
    tril = _tril(ln)
    carry = jnp.zeros((1, LANES), F32)
    for c in range(t // ln):
        rs = slice(c * ln, (c + 1) * ln)
        lf = _log_sigmoid(sm_ref[rs, :] + bias_ref[...])
        cs = _dot(tril, lf, precision=HIGHEST) + carry
        carry = cs[ln - 1:ln]
        for p in range(FOX_HEADS // 2):
            ccol_ref[rs, p * LANES:(p + 1) * LANES] = pltpu.roll(cs, LANES - (FOX_LANE0 + 2 * p), 1)


def _fox_prep(small, f_bias, bsz, t):
    bias = jnp.pad(f_bias, (FOX_LANE0, LANES - FOX_LANE0 - FOX_HEADS)).reshape(1, LANES)
    return pl.pallas_call(
        functools.partial(_fox_prep_body, t=t),
        grid=(bsz,),
        in_specs=[pl.BlockSpec((t, LANES), lambda b: (b, 1)), pl.BlockSpec((1, LANES), lambda b: (0, 0))],
        out_specs=pl.BlockSpec((t, BRANCH), lambda b: (b, 0)),
        out_shape=jax.ShapeDtypeStruct((bsz * t, BRANCH), F32),
        compiler_params=_params("parallel"),
        name="fox_prep",
    )(small, bias)


def _fox_bias_lanes(c, query_side):
    n = c.shape[0]
    hi = c.astype(BF16).astype(F32)
    rest = c - hi
    mid = rest.astype(BF16).astype(F32)
    lo = rest - mid
    lane = lax.broadcasted_iota(jnp.int32, (n, LANES), 1)
    if query_side:
        pieces = jnp.where(lane == 0, hi, jnp.where(lane == 1, mid, lo))
        out = jnp.where(lane < 3, pieces, jnp.where(lane < 6, 1.0, 0.0))
    else:
        pieces = jnp.where(lane == 3, -hi, jnp.where(lane == 4, -mid, -lo))
        out = jnp.where(lane < 3, 1.0, jnp.where(lane < 6, pieces, 0.0))
    return out.astype(BF16)


def _fox_body(q_ref, k_ref, v_ref, cq_ref, ck_ref, o_ref, kaug_ref, vt_ref, s_ref, p_ref, *, tq, t):
    qi = pl.program_id(2)
    hd = FOX_HEAD_DIM
    lane = lax.broadcasted_iota(jnp.int32, (tq, LANES), 1)
    sub = lax.broadcasted_iota(jnp.int32, (LANES, tq), 0)

    @pl.when(qi == 0)
    def _():
        for c in range(t // tq):
            rs = slice(c * tq, (c + 1) * tq)
            kb = k_ref[rs, :]
            vt = v_ref[rs, :].T
            for h in range(2):
                mine = (lane < hd) if h == 0 else (lane >= hd)
                kaug_ref[h, rs, :LANES] = jnp.where(mine, kb, jnp.zeros_like(kb))
                kaug_ref[h, rs, LANES:] = _fox_bias_lanes(ck_ref[rs, h:h + 1], False)
                mine_t = (sub < hd) if h == 0 else (sub >= hd)
                vt_ref[h, :, rs] = jnp.where(mine_t, vt, jnp.ones_like(vt))

    q = q_ref[...] * (hd ** -0.5)
    q_aug = []
    for h in range(2):
        mine = (lane < hd) if h == 0 else (lane >= hd)
        q_aug.append(jnp.concatenate([jnp.where(mine, q, jnp.zeros_like(q)),
                                      _fox_bias_lanes(cq_ref[:, h:h + 1], True)], axis=1))
    key = lax.broadcasted_iota(jnp.int32, (tq, tq), 0)
    qry = lax.broadcasted_iota(jnp.int32, (tq, tq), 1)

    for c in range(t // tq):
        @pl.when(qi == c)
        def _(c=c):
            nk = (c + 1) * tq
            accs = []
            for h in range(2):
                s_ref[h, :nk, :] = _dot_nt(kaug_ref[h, :nk, :], q_aug[h])
                diag = jnp.where(key <= qry, s_ref[h, c * tq:nk, :], -jnp.inf)
                m = jnp.max(diag, axis=0, keepdims=True)
                if c > 0:
                    m = jnp.maximum(m, jnp.max(s_ref[h, :c * tq, :], axis=0, keepdims=True))
                    p_ref[h, :c * tq, :] = jnp.exp(s_ref[h, :c * tq, :] - m).astype(BF16)
                p_ref[h, c * tq:nk, :] = jnp.exp(diag - m).astype(BF16)
                accs.append(_dot(vt_ref[h, :, :nk], p_ref[h, :nk, :]))
            a0, a1 = accs
            o_t = jnp.where(sub < hd, a0 / a0[hd:hd + 1, :], a1 / a1[0:1, :])
            o_ref[...] = o_t.T.astype(o_ref.dtype)


def _fox(qkv, ccol, bsz, t):
    tq = min(256, t)
    nq = t // tq
    pairs = FOX_HEADS // 2
    return pl.pallas_call(
        functools.partial(_fox_body, tq=tq, t=t),
        grid=(bsz, pairs, nq),
        in_specs=[
            pl.BlockSpec((tq, LANES), lambda b, p, i: (b * nq + i, p)),
            pl.BlockSpec((t, LANES), lambda b, p, i: (b, pairs + p)),
            pl.BlockSpec((t, LANES), lambda b, p, i: (b, 2 * pairs + p)),
            pl.BlockSpec((tq, LANES), lambda b, p, i: (b * nq + i, p)),
            pl.BlockSpec((t, LANES), lambda b, p, i: (b, p)),
        ],
        out_specs=pl.BlockSpec((tq, LANES), lambda b, p, i: (b * nq + i, p)),
        out_shape=jax.ShapeDtypeStruct((bsz * t, BRANCH), BF16),
        scratch_shapes=[pltpu.VMEM((2, t, 2 * LANES), BF16), pltpu.VMEM((2, LANES, t), BF16),
                        pltpu.VMEM((2, t, tq), F32), pltpu.VMEM((2, t, tq), BF16)],
        compiler_params=_params("parallel", "parallel", "arbitrary"),
        name="fox",
    )(qkv, qkv, qkv, ccol, ccol)


def _merge_body(y0_ref, y1_ref, y2_ref, w0_ref, w1_ref, w2_ref, g0_ref, g1_ref, g2_ref, o_ref):
    acc = jax.nn.sigmoid(g0_ref[...]) * _dot(y0_ref[...], w0_ref[...])
    acc = acc + jax.nn.sigmoid(g1_ref[...]) * _dot(y1_ref[...], w1_ref[...])
    acc = acc + jax.nn.sigmoid(g2_ref[...]) * _dot(y2_ref[...], w2_ref[...])
    o_ref[...] = acc.astype(o_ref.dtype)


def _merge(ys, ws, gates):
    m = gates.shape[0]
    tm = min(512, m)
    tn = 1024
    nj = D_MODEL // tn
    y_spec = pl.BlockSpec((tm, BRANCH), lambda i, j: (i, 0))
    w_spec = pl.BlockSpec((BRANCH, tn), lambda i, j: (0, j))
    g_specs = [pl.BlockSpec((tm, tn), functools.partial(lambda i, j, r: (i, r * nj + j), r=r)) for r in range(3)]
    return pl.pallas_call(
        _merge_body,
        grid=(m // tm, nj),
        in_specs=[y_spec] * 3 + [w_spec] * 3 + g_specs,
        out_specs=pl.BlockSpec((tm, tn), lambda i, j: (i, j)),
        out_shape=jax.ShapeDtypeStruct((m, D_MODEL), BF16),
        compiler_params=_params("parallel", "parallel"),
        name="merge",
    )(*ys, *ws, gates, gates, gates)


def _ffn_body(h_ref, halo_ref, x_ref, wg_ref, wu_ref, cwg_ref, cwu_ref, cbg_ref, cbu_ref, wd_ref, o_ref,
              hh_ref, acc_ref, *, tm, tiles_per_seq):
    i = pl.program_id(0)
    j = pl.program_id(1)
    pad = 16

    @pl.when(j == 0)
    def _():
        first = (i % tiles_per_seq) == 0
        halo = halo_ref[...]
        hh_ref[:pad, :] = jnp.where(first, jnp.zeros_like(halo), halo)
        hh_ref[pad:, :] = h_ref[...]
        acc_ref[...] = jnp.zeros_like(acc_ref)

    def conv(w_ref, cw_ref, cb_ref):
        u = _dot(hh_ref[...], w_ref[...])
        cw = cw_ref[...]
        out = cb_ref[...] + cw[FFN_CONV - 1:FFN_CONV] * u[pad:]
        for k in range(FFN_CONV - 1):
            back = FFN_CONV - 1 - k
            out = out + cw[k:k + 1] * u[pad - back:pad - back + tm]
        return out

    act = (_silu(conv(wg_ref, cwg_ref, cbg_ref)) * conv(wu_ref, cwu_ref, cbu_ref)).astype(BF16)
    acc_ref[...] += _dot(act, wd_ref[...])

    @pl.when(j == pl.num_programs(1) - 1)
    def _():
        o_ref[...] = x_ref[...] + acc_ref[...]


def _ffn(h, x, wg, wu, cwg, cwu, cbg, cbu, wd, t):
    m = x.shape[0]
    tm = min(512, t)
    nf = D_FF_PAD // FF_TILE
    hb = tm // 16
    col = lambda i, j: (0, j)
    return pl.pallas_call(
        functools.partial(_ffn_body, tm=tm, tiles_per_seq=t // tm),
        grid=(m // tm, nf),
        in_specs=[
            pl.BlockSpec((tm, D_MODEL), lambda i, j: (i, 0)),
            pl.BlockSpec((16, D_MODEL), lambda i, j: (jnp.maximum(i * hb - 1, 0), 0)),
            pl.BlockSpec((tm, D_MODEL), lambda i, j: (i, 0)),
            pl.BlockSpec((D_MODEL, FF_TILE), col),
            pl.BlockSpec((D_MODEL, FF_TILE), col),
            pl.BlockSpec((FFN_CONV, FF_TILE), col),
            pl.BlockSpec((FFN_CONV, FF_TILE), col),
            pl.BlockSpec((1, FF_TILE), col),
            pl.BlockSpec((1, FF_TILE), col),
            pl.BlockSpec((FF_TILE, D_MODEL), lambda i, j: (j, 0)),
        ],
        out_specs=pl.BlockSpec((tm, D_MODEL), lambda i, j: (i, 0)),
        out_shape=jax.ShapeDtypeStruct((m, D_MODEL), F32),
        scratch_shapes=[pltpu.VMEM((16 + tm, D_MODEL), BF16), pltpu.VMEM((tm, D_MODEL), F32)],
        compiler_params=_params("parallel", "arbitrary"),
        name="ffn",
    )(h, h, x, wg, wu, cwg, cwu, cbg, cbu, wd)


def _ffn_weights(w_up, conv_w, conv_b, w_down):
    padc = lambda a: jnp.pad(a, [(0, 0)] * (a.ndim - 1) + [(0, D_FF_PAD - D_FF)])
    halves = lambda a: (padc(a[..., :D_FF]), padc(a[..., D_FF:]))
    wg, wu = halves(w_up)
    cwg, cwu = halves(conv_w)
    cbg, cbu = halves(conv_b.reshape(1, -1))
    wd = jnp.pad(w_down, ((0, D_FF_PAD - D_FF), (0, 0))).astype(BF16)
    return wg.astype(BF16), wu.astype(BF16), cwg, cwu, cbg, cbu, wd


def _in_proj_weights(w):
    wt = w.T
    offs = _IN_OFFS
    seg = lambda a, b: wt[offs[a]:offs[b]]
    window = lambda i: wt[offs[i] - offs[i] % LANES:offs[i] - offs[i] % LANES + LANES]
    small = jnp.concatenate([window(2), window(10)], axis=0)
    cast = lambda a: a.astype(BF16)
    return dict(z=cast(seg(0, 1)), xbc=cast(seg(1, 2)), hgrn=cast(seg(3, 7)), fox=cast(seg(7, 10)),
                gates=cast(seg(11, 12)), small=cast(small))


def kernel(x, norm_mix_w, w_in, ssd_conv_w, ssd_conv_b, ssd_dt_bias, ssd_a_log, ssd_d, ssd_norm_w, hgrn_lb, hgrn_norm_w, fox_f_bias, w_branch_ssd, w_branch_hgrn, w_branch_fox, w_out, norm_ffn_w, ffn_w_up, ffn_conv_w, ffn_conv_b, ffn_w_down, final_norm_w):
    bsz, t, d = x.shape
    depth = w_in.shape[0]
    lbs = jnp.cumsum(jax.nn.softmax(hgrn_lb.astype(F32), axis=0), axis=0)
    lbs = lbs - lbs[0]
    x = x.reshape(bsz * t, d)
    for l in range(depth):
        wi = _in_proj_weights(w_in[l])
        h = _rmsnorm(x, norm_mix_w[l], BF16)
        proj = lambda key, dtype, tn: _mm(h, wi[key], dtype, tn, "proj_" + key, b_is_nk=True)
        z = proj("z", F32, 1024)
        xbc = proj("xbc", F32, SSD_XBC)
        small = proj("small", F32, 2 * LANES)
        p_hgrn = proj("hgrn", F32, 1024)
        p_fox = proj("fox", BF16, 1024)
        gates = proj("gates", F32, 1024)
        y_ssd = _ssd(z, xbc, small, ssd_conv_w[l], ssd_conv_b[l], ssd_dt_bias[l], ssd_a_log[l], ssd_d[l],
                     ssd_norm_w[l], bsz, t)
        y_hgrn = _hgrn(p_hgrn, lbs[l], hgrn_norm_w[l], bsz, t)
        y_fox = _fox(p_fox, _fox_prep(small, fox_f_bias[l], bsz, t), bsz, t)
        merged = _merge((y_ssd, y_hgrn, y_fox),
                        (w_branch_ssd[l].astype(BF16), w_branch_hgrn[l].astype(BF16), w_branch_fox[l].astype(BF16)),
                        gates)
        x = _mm(merged, w_out[l].astype(BF16), F32, 1024, "out_proj", residual=x)
        h = _rmsnorm(x, norm_ffn_w[l], BF16)
        x = _ffn(h, x, *_ffn_weights(ffn_w_up[l], ffn_conv_w[l], ffn_conv_b[l], ffn_w_down[l]), t)
    return _rmsnorm(x, final_norm_w, F32).reshape(bsz, t, d)
```

```python
import functools

import jax
import jax.numpy as jnp
import numpy as np
from jax import lax
from jax.experimental import pallas as pl
from jax.experimental.pallas import tpu as pltpu

F32 = jnp.float32
BF16 = jnp.bfloat16
HIGHEST = lax.Precision.HIGHEST
LOG2E = 1.4426950408889634

D_MODEL = 2048
BRANCH = 1024
SSD_HEADS = 16
SSD_HEAD_DIM = 64
SSD_GROUPS = 2
SSD_STATE = 128
SSD_CONV = 4
SSD_CHUNK = 128
SSD_XBC = BRANCH + 2 * SSD_GROUPS * SSD_STATE
HGRN_HEADS = 8
HGRN_DK = 128
HGRN_CHUNK = 64
HGRN_SUB = 8
FOX_HEADS = 16
FOX_HEAD_DIM = 64
D_FF = 5504
FFN_CONV = 3
NORM_EPS = 1e-6
IN_SIZES = (BRANCH, SSD_XBC, SSD_HEADS, BRANCH, BRANCH, BRANCH, BRANCH, BRANCH, BRANCH, BRANCH,
            FOX_HEADS, 3 * D_MODEL)

LANES = 128
FF_TILE = 512
FF_SUB = 256
D_FF_PAD = 5632
VMEM_LIMIT = 56 * 1024 * 1024

_IN_OFFS = [sum(IN_SIZES[:i]) for i in range(len(IN_SIZES) + 1)]
DT_LANE0 = _IN_OFFS[2] % LANES
FOX_LANE0 = _IN_OFFS[10] % LANES
assert DT_LANE0 == 0 and FOX_LANE0 + FOX_HEADS <= LANES


def _params(*sem):
    return pltpu.CompilerParams(dimension_semantics=sem, vmem_limit_bytes=VMEM_LIMIT)


def _silu(x):
    return x * jax.nn.sigmoid(x)


def _log_sigmoid(x):
    return jnp.minimum(x, 0.0) - jnp.log(1.0 + jnp.exp(-jnp.abs(x)))


def _dot(a, b, **kw):
    return jnp.dot(a, b, preferred_element_type=F32, **kw)


def _dot_nt(a, b):
    return lax.dot_general(a, b, (((1,), (1,)), ((), ())), preferred_element_type=F32)


def _dot_tn(a, b):
    return lax.dot_general(a, b, (((0,), (0,)), ((), ())), preferred_element_type=F32)


def _tril(n):
    r = lax.broadcasted_iota(jnp.int32, (n, n), 0)
    c = lax.broadcasted_iota(jnp.int32, (n, n), 1)
    return (r >= c).astype(F32)


def _rmsnorm_body(x_ref, w_ref, o_ref):
    x = x_ref[...]
    ms = jnp.mean(x * x, axis=-1, keepdims=True)
    o_ref[...] = (x * lax.rsqrt(ms + NORM_EPS) * w_ref[...]).astype(o_ref.dtype)


def _rmsnorm(x, w, out_dtype):
    m, d = x.shape
    tm = min(512, m)
    return pl.pallas_call(
        _rmsnorm_body,
        grid=(m // tm,),
        in_specs=[pl.BlockSpec((tm, d), lambda i: (i, 0)), pl.BlockSpec((1, d), lambda i: (0, 0))],
        out_specs=pl.BlockSpec((tm, d), lambda i: (i, 0)),
        out_shape=jax.ShapeDtypeStruct((m, d), out_dtype),
        compiler_params=_params("parallel"),
        name="rmsnorm",
    )(x, w.reshape(1, d))


def _mm_body(a_ref, b_ref, o_ref):
    o_ref[...] = _dot(a_ref[...], b_ref[...]).astype(o_ref.dtype)


def _mm_nt_body(a_ref, b_ref, o_ref):
    o_ref[...] = _dot_nt(a_ref[...], b_ref[...]).astype(o_ref.dtype)


def _mm_res_body(a_ref, b_ref, r_ref, o_ref):
    o_ref[...] = r_ref[...] + _dot(a_ref[...], b_ref[...])


def _mm_t_body(w_ref, a_ref, o_ref):
    o_ref[...] = _dot_nt(w_ref[...], a_ref[...]).astype(o_ref.dtype)


def _mm_t(w, a, out_dtype, name, tm=1024):
    n, k = w.shape
    m = a.shape[0]
    tm = min(tm, m)
    return pl.pallas_call(
        _mm_t_body,
        grid=(m // tm,),
        in_specs=[pl.BlockSpec((n, k), lambda i: (0, 0), pipeline_mode=pl.Buffered(1)),
                  pl.BlockSpec((tm, k), lambda i: (i, 0))],
        out_specs=pl.BlockSpec((n, tm), lambda i: (0, i)),
        out_shape=jax.ShapeDtypeStruct((n, m), out_dtype),
        compiler_params=_params("parallel"),
        name=name,
    )(w, a)


def _mm(a, b, out_dtype, tn, name, residual=None, b_is_nk=False, tm=1024):
    m, k = a.shape
    n = b.shape[0] if b_is_nk else b.shape[1]
    tm = min(tm, m)
    mode = dict(pipeline_mode=pl.Buffered(1)) if n == tn else {}
    b_spec = (pl.BlockSpec((tn, k), lambda i, j: (j, 0), **mode) if b_is_nk
              else pl.BlockSpec((k, tn), lambda i, j: (0, j), **mode))
    in_specs = [pl.BlockSpec((tm, k), lambda i, j: (i, 0)), b_spec]
    args = [a, b]
    body = _mm_nt_body if b_is_nk else _mm_body
    if residual is not None:
        in_specs.append(pl.BlockSpec((tm, tn), lambda i, j: (i, j)))
        args.append(residual)
        body = _mm_res_body
    return pl.pallas_call(
        body,
        grid=(m // tm, n // tn),
        in_specs=in_specs,
        out_specs=pl.BlockSpec((tm, tn), lambda i, j: (i, j)),
        out_shape=jax.ShapeDtypeStruct((m, n), out_dtype),
        compiler_params=_params("parallel", "parallel"),
        name=name,
    )(*args)


def _ssd_body(z_ref, xbc_ref, sm_ref, cw_ref, cb_ref, dtb_ref, alog_ref, d_ref, nw_ref, o_ref,
              halo_ref, s_ref, *, tt):
    @pl.when(pl.program_id(1) == 0)
    def _():
        halo_ref[...] = jnp.zeros_like(halo_ref)
        s_ref[...] = jnp.zeros_like(s_ref)

    xbc = xbc_ref[...]
    ext = jnp.concatenate([halo_ref[...], xbc], axis=0)
    halo_ref[...] = xbc[tt - 8:, :]
    cw = cw_ref[...]
    conv = cb_ref[...] + cw[SSD_CONV - 1:SSD_CONV] * xbc
    for k in range(SSD_CONV - 1):
        back = SSD_CONV - 1 - k
        conv = conv + cw[k:k + 1] * ext[8 - back:8 - back + tt]
    act = _silu(conv)
    xs_all = act[:, :BRANCH]
    dt_all = jax.nn.softplus(sm_ref[...] + dtb_ref[...])
    adt_all = dt_all * (-jnp.exp(alog_ref[...]))

    ln = SSD_CHUNK
    tril = _tril(ln)
    row = lax.broadcasted_iota(jnp.int32, (ln, ln), 0)
    col = lax.broadcasted_iota(jnp.int32, (ln, ln), 1)
    causal = row >= col
    lane = lax.broadcasted_iota(jnp.int32, (ln, LANES), 1)
    eh = lax.broadcasted_iota(jnp.int32, (LANES, BRANCH), 0)
    ec = lax.broadcasted_iota(jnp.int32, (LANES, BRANCH), 1)
    expand = (ec // SSD_HEAD_DIM == eh).astype(F32)
    gw = BRANCH // SSD_GROUPS

    for c in range(tt // ln):
        rs = slice(c * ln, (c + 1) * ln)
        xs = xs_all[rs]
        cs = _dot(tril, adt_all[rs], precision=HIGHEST)
        cs_t = cs.T
        cs_x = _dot(cs, expand, precision=HIGHEST)
        dt_x = _dot(dt_all[rs], expand, precision=HIGHEST)
        xdt = xs * dt_x
        last_x = cs_x[ln - 1:ln]
        xdt_st = (xdt * jnp.exp(last_x - cs_x)).astype(BF16)
        out_decay = jnp.exp(cs_x)
        chunk_decay = jnp.exp(last_x)
        xdt_b = xdt.astype(BF16)
        ys = []
        for g in range(SSD_GROUPS):
            b_g = act[rs, BRANCH + g * SSD_STATE:BRANCH + (g + 1) * SSD_STATE].astype(BF16)
            c_g = act[rs, BRANCH + (SSD_GROUPS + g) * SSD_STATE:
                      BRANCH + (SSD_GROUPS + g + 1) * SSD_STATE].astype(BF16)
            gs = slice(g * gw, (g + 1) * gw)
            cb = _dot_nt(c_g, b_g)
            state = s_ref[g]
            y_off = _dot(c_g, state.astype(BF16)) * out_decay[:, gs]
            s_ref[g] = state * chunk_decay[:, gs] + _dot_tn(b_g, xdt_st[:, gs])
            heads_per_group = SSD_HEADS // SSD_GROUPS
            pieces = []
            for p in range(heads_per_group // 2):
                h0 = g * heads_per_group + 2 * p
                xp = xdt_b[:, h0 * SSD_HEAD_DIM:(h0 + 2) * SSD_HEAD_DIM]
                res = []
                for h in (h0, h0 + 1):
                    seg = jnp.where(causal, cs[:, h:h + 1] - cs_t[h:h + 1, :], -jnp.inf)
                    res.append(_dot((cb * jnp.exp(seg)).astype(BF16), xp))
                pieces.append(jnp.where(lane < SSD_HEAD_DIM, res[0], res[1]))
            ys.append(jnp.concatenate(pieces, axis=1) + y_off)
        y = jnp.concatenate(ys, axis=1) + xs * d_ref[...]
        y = y * _silu(z_ref[rs, :])
        normed = []
        for g in range(SSD_GROUPS):
            yg = y[:, g * gw:(g + 1) * gw]
            ms = jnp.mean(yg * yg, axis=-1, keepdims=True)
            normed.append(yg * lax.rsqrt(ms + NORM_EPS))
        o_ref[rs, :] = (jnp.concatenate(normed, axis=1) * nw_ref[...]).astype(o_ref.dtype)


def _ssd(z, xbc, small, conv_w, conv_b, dt_bias, a_log, d_skip, norm_w, bsz, t):
    tt = min(512, t)
    nt = t // tt
    row = lambda b, i: (b * nt + i, 0)
    fixed = lambda b, i: (0, 0)
    pad = lambda v: jnp.pad(v, (0, LANES - SSD_HEADS)).reshape(1, LANES)
    return pl.pallas_call(
        functools.partial(_ssd_body, tt=tt),
        grid=(bsz, nt),
        in_specs=[
            pl.BlockSpec((tt, BRANCH), row),
            pl.BlockSpec((tt, SSD_XBC), row),
            pl.BlockSpec((tt, LANES), row),
            pl.BlockSpec((SSD_CONV, SSD_XBC), fixed),
            pl.BlockSpec((1, SSD_XBC), fixed),
            pl.BlockSpec((1, LANES), fixed),
            pl.BlockSpec((1, LANES), fixed),
            pl.BlockSpec((1, BRANCH), fixed),
            pl.BlockSpec((1, BRANCH), fixed),
        ],
        out_specs=pl.BlockSpec((tt, BRANCH), row),
        out_shape=jax.ShapeDtypeStruct((bsz * t, BRANCH), BF16),
        scratch_shapes=[pltpu.VMEM((8, SSD_XBC), F32),
                        pltpu.VMEM((SSD_GROUPS, SSD_STATE, BRANCH // SSD_GROUPS), F32)],
        compiler_params=_params("parallel", "arbitrary"),
        name="ssd",
    )(z, xbc, small, conv_w, conv_b.reshape(1, -1), pad(dt_bias), pad(a_log),
      jnp.repeat(d_skip, SSD_HEAD_DIM).reshape(1, -1), norm_w.reshape(1, -1))


def _hgrn_body(x_ref, loglb_ref, log1m_ref, onem_ref, nw_ref, o_ref, st_ref, *, tt):
    @pl.when(pl.program_id(1) == 0)
    def _():
        st_ref[...] = jnp.zeros_like(st_ref)

    cl, sub = HGRN_CHUNK, HGRN_SUB
    nsub = cl // sub
    tril = _tril(cl)
    t_idx = lax.broadcasted_iota(jnp.int32, (sub, HGRN_DK), 0)
    lane = lax.broadcasted_iota(jnp.int32, (sub, cl), 1)

    def chunk(c, carry):
        r0 = pl.multiple_of(c * cl, cl)
        hq = x_ref[pl.ds(r0, cl), 0 * BRANCH:1 * BRANCH]
        hf = x_ref[pl.ds(r0, cl), 1 * BRANCH:2 * BRANCH]
        hi = x_ref[pl.ds(r0, cl), 2 * BRANCH:3 * BRANCH]
        hg = x_ref[pl.ds(r0, cl), 3 * BRANCH:4 * BRANCH]
        e = jnp.exp(-jnp.abs(hf))
        d = 1.0 + e
        kk = onem_ref[...] * (jnp.where(hf >= 0.0, e, 1.0) / d)
        la = loglb_ref[...]
        lb2 = log1m_ref[...] + (jnp.minimum(hf, 0.0) - jnp.log(d))
        log_f = jnp.maximum(la, lb2) + jnp.log(1.0 + jnp.exp(-jnp.abs(la - lb2)))
        q = _silu(hq)
        b = _dot(tril, log_f * LOG2E, precision=HIGHEST)
        b_last = b[cl - 1:cl]
        q_in = (q * jnp.exp2(b)).astype(BF16)
        k_st = (kk * jnp.exp2(b_last - b)).astype(BF16)
        dec = jnp.exp2(b_last)
        v_b = hi.astype(BF16)
        outs = []
        for h in range(HGRN_HEADS):
            sl = slice(h * HGRN_DK, (h + 1) * HGRN_DK)
            bh, qh, kh, vh = b[:, sl], q[:, sl], kk[:, sl], v_b[:, sl]
            st = st_ref[h]
            o_h = _dot_nt(q_in[:, sl], st.astype(BF16))
            st_ref[h] = st * dec[:, sl] + _dot_tn(vh, k_st[:, sl])
            a_rows = []
            for i in range(nsub):
                i0 = i * sub
                bi, qi = bh[i0:i0 + sub], qh[i0:i0 + sub]
                lhs, rhs = [], []
                if i > 0:
                    beta = bh[i0 - 1:i0]
                    lhs.append(qi * jnp.exp2(bi - beta))
                    rhs.append(kh[:i0] * jnp.exp2(beta - bh[:i0]))
                for s in range(sub):
                    lhs.append(qi * jnp.exp2(jnp.where(t_idx >= s, bi - bi[s:s + 1], -jnp.inf)))
                rhs.append(kh[i0:i0 + sub])
                if i0 + sub < cl:
                    rhs.append(jnp.zeros((cl - i0 - sub, HGRN_DK), F32))
                res = _dot_nt(jnp.concatenate(lhs, axis=0).astype(BF16),
                              jnp.concatenate(rhs, axis=0).astype(BF16))
                off = sub if i > 0 else 0
                a_i = jnp.where(lane < i0, res[:sub], 0.0) if i > 0 else jnp.zeros((sub, cl), F32)
                for s in range(sub):
                    a_i = a_i + jnp.where(lane == i0 + s, res[off + s * sub:off + (s + 1) * sub], 0.0)
                a_rows.append(a_i)
            o_h = o_h + _dot(jnp.concatenate(a_rows, axis=0).astype(BF16), vh)
            ms = jnp.mean(o_h * o_h, axis=-1, keepdims=True)
            outs.append(o_h * lax.rsqrt(ms + NORM_EPS))
        o = jnp.concatenate(outs, axis=1) * nw_ref[...]
        o_ref[pl.ds(r0, cl), :] = (o * _silu(hg)).astype(o_ref.dtype)
        return carry

    lax.fori_loop(0, tt // cl, chunk, 0)


def _hgrn(proj, lb, norm_w, bsz, t):
    tt = min(256, t)
    nt = t // tt
    row = lambda b, i: (b * nt + i, 0)
    fixed = lambda b, i: (0, 0)
    lb = lb.reshape(1, BRANCH)
    return pl.pallas_call(
        functools.partial(_hgrn_body, tt=tt),
        grid=(bsz, nt),
        in_specs=[pl.BlockSpec((tt, 4 * BRANCH), row)] + [pl.BlockSpec((1, BRANCH), fixed)] * 4,
        out_specs=pl.BlockSpec((tt, BRANCH), row),
        out_shape=jax.ShapeDtypeStruct((bsz * t, BRANCH), BF16),
        scratch_shapes=[pltpu.VMEM((HGRN_HEADS, HGRN_DK, HGRN_DK), F32)],
        compiler_params=_params("parallel", "arbitrary"),
        name="hgrn",
    )(proj, jnp.log(lb), jnp.log1p(-lb), 1.0 - lb, norm_w.reshape(1, BRANCH))


FOX_BIAS_STRIDE = 8


def _fox_bias_tables():
    sel_q = np.zeros((3 * LANES, BRANCH), np.float32)
    sel_k = np.zeros((3 * LANES, BRANCH), np.float32)
    one_q = np.zeros((1, BRANCH), np.float32)
    one_k = np.zeros((1, BRANCH), np.float32)
    for h in range(FOX_HEADS):
        base = (h // 2) * LANES + FOX_BIAS_STRIDE * (h % 2)
        for j in range(3):
            sel_q[j * LANES + FOX_LANE0 + h, base + j] = 1.0
            sel_k[j * LANES + FOX_LANE0 + h, base + 3 + j] = -1.0
            one_q[0, base + 3 + j] = 1.0
            one_k[0, base + j] = 1.0
    return jnp.asarray(sel_q, BF16), jnp.asarray(sel_k, BF16), jnp.asarray(one_q), jnp.asarray(one_k)


def _fox_prep_body(sm_ref, bias_ref, selq_ref, selk_ref, oneq_ref, onek_ref, qb_ref, kb_ref, pieces_ref, *, t):
    ln = LANES
    tril = _tril(ln)
    carry = jnp.zeros((1, LANES), F32)
    for c in range(t // ln):
        rs = slice(c * ln, (c + 1) * ln)
        lf = _log_sigmoid(sm_ref[rs, :] + bias_ref[...])
        cs = _dot(tril, lf, precision=HIGHEST) + carry
        carry = cs[ln - 1:ln]
        hi = cs.astype(BF16)
        rest = cs - hi.astype(F32)
        mid = rest.astype(BF16)
        lo = (rest - mid.astype(F32)).astype(BF16)
        pieces_ref[rs, :] = jnp.concatenate([hi, mid, lo], axis=1)
    pieces = pieces_ref[...]
    qb_ref[...] = (_dot(pieces, selq_ref[...]) + oneq_ref[...]).astype(BF16)
    kb_ref[...] = (_dot(pieces, selk_ref[...]) + onek_ref[...]).astype(BF16)


def _fox_prep(small, f_bias, bsz, t):
    bias = jnp.pad(f_bias, (FOX_LANE0, LANES - FOX_LANE0 - FOX_HEADS)).reshape(1, LANES)
    fixed = lambda b: (0, 0)
    out = jax.ShapeDtypeStruct((bsz * t, BRANCH), BF16)
    return pl.pallas_call(
        functools.partial(_fox_prep_body, t=t),
        grid=(bsz,),
        in_specs=[pl.BlockSpec((t, LANES), lambda b: (b, 1)), pl.BlockSpec((1, LANES), fixed),
                  pl.BlockSpec((3 * LANES, BRANCH), fixed), pl.BlockSpec((3 * LANES, BRANCH), fixed),
                  pl.BlockSpec((1, BRANCH), fixed), pl.BlockSpec((1, BRANCH), fixed)],
        out_specs=[pl.BlockSpec((t, BRANCH), lambda b: (b, 0))] * 2,
        out_shape=[out, out],
        scratch_shapes=[pltpu.VMEM((t, 3 * LANES), BF16)],
        compiler_params=_params("parallel"),
        name="fox_prep",
    )(small, bias, *_fox_bias_tables())


FOX_ONES_ROWS = 16


def _fox_body(q_ref, k_ref, vt_in_ref, qb_ref, kb_ref, o_ref, kaug_ref, vt_ref, s_ref, p_ref, *, tq, t):
    qi = pl.program_id(2)
    hd = FOX_HEAD_DIM
    lane = lax.broadcasted_iota(jnp.int32, (tq, LANES), 1)
    sub = lax.broadcasted_iota(jnp.int32, (LANES, tq), 0)

    @pl.when(qi == 0)
    def _():
        kaug_ref[:, :LANES] = k_ref[...]
        kaug_ref[:, LANES:] = kb_ref[...]
        vt_ref[:LANES, :] = vt_in_ref[...]
        vt_ref[LANES:, :] = jnp.ones((FOX_ONES_ROWS, t), BF16)

    q = q_ref[...] * (hd ** -0.5)
    qb = qb_ref[...]
    q_aug = []
    for h in range(2):
        mine = (lane < hd) if h == 0 else (lane >= hd)
        mine_b = (lane < FOX_BIAS_STRIDE) if h == 0 else (lane >= FOX_BIAS_STRIDE)
        q_aug.append(jnp.concatenate([jnp.where(mine, q, jnp.zeros_like(q)),
                                      jnp.where(mine_b, qb, jnp.zeros_like(qb))], axis=1))
    q_aug = jnp.concatenate(q_aug, axis=0)
    key = lax.broadcasted_iota(jnp.int32, (tq, 2 * tq), 0)
    qry = lax.broadcasted_iota(jnp.int32, (tq, 2 * tq), 1) % tq

    for c in range(t // tq):
        @pl.when(qi == c)
        def _(c=c):
            nk = (c + 1) * tq
            s_ref[:nk, :] = _dot_nt(kaug_ref[:nk, :], q_aug)
            diag = jnp.where(key <= qry, s_ref[c * tq:nk, :], -jnp.inf)
            m = jnp.max(diag, axis=0, keepdims=True)
            if c > 0:
                m = jnp.maximum(m, jnp.max(s_ref[:c * tq, :], axis=0, keepdims=True))
                p_ref[:c * tq, :] = jnp.exp(s_ref[:c * tq, :] - m).astype(BF16)
            p_ref[c * tq:nk, :] = jnp.exp(diag - m).astype(BF16)
            acc = _dot(vt_ref[:, :nk], p_ref[:nk, :])
            o0 = acc[:LANES, :tq] / acc[LANES:LANES + 1, :tq]
            o1 = acc[:LANES, tq:] / acc[LANES:LANES + 1, tq:]
            o_ref[...] = jnp.where(sub < hd, o0, o1).T.astype(o_ref.dtype)


def _fox(qk, vt, qbias, kbias, bsz, t):
    tq = min(256, t)
    nq = t // tq
    pairs = FOX_HEADS // 2
    return pl.pallas_call(
        functools.partial(_fox_body, tq=tq, t=t),
        grid=(bsz, pairs, nq),
        in_specs=[
            pl.BlockSpec((tq, LANES), lambda b, p, i: (b * nq + i, p)),
            pl.BlockSpec((t, LANES), lambda b, p, i: (b, pairs + p)),
            pl.BlockSpec((LANES, t), lambda b, p, i: (p, b)),
            pl.BlockSpec((tq, LANES), lambda b, p, i: (b * nq + i, p)),
            pl.BlockSpec((t, LANES), lambda b, p, i: (b, p)),
        ],
        out_specs=pl.BlockSpec((tq, LANES), lambda b, p, i: (b * nq + i, p)),
        out_shape=jax.ShapeDtypeStruct((bsz * t, BRANCH), BF16),
        scratch_shapes=[pltpu.VMEM((t, 2 * LANES), BF16), pltpu.VMEM((LANES + FOX_ONES_ROWS, t), BF16),
                        pltpu.VMEM((t, 2 * tq), F32), pltpu.VMEM((t, 2 * tq), BF16)],
        compiler_params=_params("parallel", "parallel", "arbitrary"),
        name="fox",
    )(qk, qk, vt, qbias, kbias)


def _merge_body(y0_ref, y1_ref, y2_ref, w0_ref, w1_ref, w2_ref, g0_ref, g1_ref, g2_ref, o_ref):
    acc = jax.nn.sigmoid(g0_ref[...]) * _dot(y0_ref[...], w0_ref[...])
    acc = acc + jax.nn.sigmoid(g1_ref[...]) * _dot(y1_ref[...], w1_ref[...])
    acc = acc + jax.nn.sigmoid(g2_ref[...]) * _dot(y2_ref[...], w2_ref[...])
    o_ref[...] = acc.astype(o_ref.dtype)


def _merge(ys, ws, gates):
    m = gates.shape[0]
    tm = min(512, m)
    y_spec = pl.BlockSpec((tm, BRANCH), lambda i: (i, 0))
    w_spec = pl.BlockSpec((BRANCH, D_MODEL), lambda i: (0, 0), pipeline_mode=pl.Buffered(1))
    g_specs = [pl.BlockSpec((tm, D_MODEL), functools.partial(lambda i, r: (i, r), r=r)) for r in range(3)]
    return pl.pallas_call(
        _merge_body,
        grid=(m // tm,),
        in_specs=[y_spec] * 3 + [w_spec] * 3 + g_specs,
        out_specs=pl.BlockSpec((tm, D_MODEL), lambda i: (i, 0)),
        out_shape=jax.ShapeDtypeStruct((m, D_MODEL), BF16),
        compiler_params=_params("parallel"),
        name="merge",
    )(*ys, *ws, gates, gates, gates)


def _ffn_body(h_ref, halo_ref, x_ref, wg_ref, wu_ref, cwg_ref, cwu_ref, cbg_ref, cbu_ref, wd_ref, o_ref,
              hh_ref, *, tm, tiles_per_seq):
    i = pl.program_id(0)
    j = pl.program_id(1)
    pad = 16

    @pl.when(j == 0)
    def _():
        first = (i % tiles_per_seq) == 0
        halo = halo_ref[...]
        hh_ref[:pad, :] = jnp.where(first, jnp.zeros_like(halo), halo)
        hh_ref[pad:, :] = h_ref[...]
        o_ref[...] = x_ref[...]

    def conv(w_ref, cw_ref, cb_ref, cs):
        u = _dot(hh_ref[...], w_ref[:, cs])
        cw = cw_ref[:, cs]
        out = cb_ref[:, cs] + cw[FFN_CONV - 1:FFN_CONV] * u[pad:]
        for k in range(FFN_CONV - 1):
            back = FFN_CONV - 1 - k
            out = out + cw[k:k + 1] * u[pad - back:pad - back + tm]
        return out

    down = None
    for s0 in range(0, FF_TILE, FF_SUB):
        cs = slice(s0, s0 + FF_SUB)
        act = (_silu(conv(wg_ref, cwg_ref, cbg_ref, cs)) * conv(wu_ref, cwu_ref, cbu_ref, cs)).astype(BF16)
        part = _dot(act, wd_ref[cs, :])
        down = part if down is None else down + part
    o_ref[...] += down


def _ffn(h, x, wg, wu, cwg, cwu, cbg, cbu, wd, t):
    m = x.shape[0]
    tm = min(1024, t)
    nf = D_FF_PAD // FF_TILE
    hb = tm // 16
    col = lambda i, j: (0, j)
    once = dict(pipeline_mode=pl.Buffered(1))
    return pl.pallas_call(
        functools.partial(_ffn_body, tm=tm, tiles_per_seq=t // tm),
        grid=(m // tm, nf),
        in_specs=[
            pl.BlockSpec((tm, D_MODEL), lambda i, j: (i, 0)),
            pl.BlockSpec((16, D_MODEL), lambda i, j: (jnp.maximum(i * hb - 1, 0), 0)),
            pl.BlockSpec((tm, D_MODEL), lambda i, j: (i, 0), **once),
            pl.BlockSpec((D_MODEL, FF_TILE), col),
            pl.BlockSpec((D_MODEL, FF_TILE), col),
            pl.BlockSpec((FFN_CONV, FF_TILE), col),
            pl.BlockSpec((FFN_CONV, FF_TILE), col),
            pl.BlockSpec((1, FF_TILE), col),
            pl.BlockSpec((1, FF_TILE), col),
            pl.BlockSpec((FF_TILE, D_MODEL), lambda i, j: (j, 0)),
        ],
        out_specs=pl.BlockSpec((tm, D_MODEL), lambda i, j: (i, 0), **once),
        out_shape=jax.ShapeDtypeStruct((m, D_MODEL), F32),
        scratch_shapes=[pltpu.VMEM((16 + tm, D_MODEL), BF16)],
        compiler_params=_params("parallel", "arbitrary"),
        name="ffn",
    )(h, h, x, wg, wu, cwg, cwu, cbg, cbu, wd)


def _ffn_weights(w_up, conv_w, conv_b, w_down):
    padc = lambda a: jnp.pad(a, [(0, 0)] * (a.ndim - 1) + [(0, D_FF_PAD - D_FF)])
    halves = lambda a: (padc(a[..., :D_FF]), padc(a[..., D_FF:]))
    wg, wu = halves(w_up)
    cwg, cwu = halves(conv_w)
    cbg, cbu = halves(conv_b.reshape(1, -1))
    wd = jnp.pad(w_down, ((0, D_FF_PAD - D_FF), (0, 0))).astype(BF16)
    return wg.astype(BF16), wu.astype(BF16), cwg, cwu, cbg, cbu, wd


def _in_proj_weights(w):
    wt = w.T
    offs = _IN_OFFS
    seg = lambda a, b: wt[offs[a]:offs[b]]
    window = lambda i: wt[offs[i] - offs[i] % LANES:offs[i] - offs[i] % LANES + LANES]
    small = jnp.concatenate([window(2), window(10)], axis=0)
    cast = lambda a: a.astype(BF16)
    return dict(z=cast(seg(0, 1)), xbc=cast(seg(1, 2)), hgrn=cast(seg(3, 7)), fox_qk=cast(seg(7, 9)),
                fox_v=cast(seg(9, 10)), gates=cast(seg(11, 12)), small=cast(small))


def kernel(x, norm_mix_w, w_in, ssd_conv_w, ssd_conv_b, ssd_dt_bias, ssd_a_log, ssd_d, ssd_norm_w, hgrn_lb, hgrn_norm_w, fox_f_bias, w_branch_ssd, w_branch_hgrn, w_branch_fox, w_out, norm_ffn_w, ffn_w_up, ffn_conv_w, ffn_conv_b, ffn_w_down, final_norm_w):
    bsz, t, d = x.shape
    depth = w_in.shape[0]
    lbs = jnp.cumsum(jax.nn.softmax(hgrn_lb.astype(F32), axis=0), axis=0)
    lbs = lbs - lbs[0]
    x = x.reshape(bsz * t, d)
    for l in range(depth):
        wi = _in_proj_weights(w_in[l])
        h = _rmsnorm(x, norm_mix_w[l], BF16)
        proj = lambda key, dtype, tn: _mm(h, wi[key], dtype, tn, "proj_" + key, b_is_nk=True)
        z = proj("z", F32, 1024)
        xbc = proj("xbc", F32, SSD_XBC)
        small = proj("small", F32, 2 * LANES)
        p_hgrn = proj("hgrn", F32, 1024)
        fox_qk = proj("fox_qk", BF16, 1024)
        fox_vt = _mm_t(wi["fox_v"], h, BF16, "proj_fox_vt")
        gates = proj("gates", F32, 1024)
        y_ssd = _ssd(z, xbc, small, ssd_conv_w[l], ssd_conv_b[l], ssd_dt_bias[l], ssd_a_log[l], ssd_d[l],
                     ssd_norm_w[l], bsz, t)
        y_hgrn = _hgrn(p_hgrn, lbs[l], hgrn_norm_w[l], bsz, t)
        y_fox = _fox(fox_qk, fox_vt, *_fox_prep(small, fox_f_bias[l], bsz, t), bsz, t)
        merged = _merge((y_ssd, y_hgrn, y_fox),
                        (w_branch_ssd[l].astype(BF16), w_branch_hgrn[l].astype(BF16), w_branch_fox[l].astype(BF16)),
                        gates)
        x = _mm(merged, w_out[l].astype(BF16), F32, D_MODEL, "out_proj", residual=x, tm=512)
        h = _rmsnorm(x, norm_ffn_w[l], BF16)
        x = _ffn(h, x, *_ffn_weights(ffn_w_up[l], ffn_conv_w[l], ffn_conv_b[l], ffn_w_down[l]), t)
    return _rmsnorm(x, final_norm_w, F32).reshape(bsz, t, d)
```

```python
import functools

import jax
import jax.numpy as jnp
import numpy as np
from jax import lax
from jax.experimental import pallas as pl
from jax.experimental.pallas import tpu as pltpu

F32 = jnp.float32
BF16 = jnp.bfloat16
HIGHEST = lax.Precision.HIGHEST
LOG2E = 1.4426950408889634

D_MODEL = 2048
BRANCH = 1024
SSD_HEADS = 16
SSD_HEAD_DIM = 64
SSD_GROUPS = 2
SSD_STATE = 128
SSD_CONV = 4
SSD_CHUNK = 128
SSD_XBC = BRANCH + 2 * SSD_GROUPS * SSD_STATE
HGRN_HEADS = 8
HGRN_DK = 128
HGRN_CHUNK = 64
HGRN_SUB = 8
FOX_HEADS = 16
FOX_HEAD_DIM = 64
D_FF = 5504
FFN_CONV = 3
NORM_EPS = 1e-6
IN_SIZES = (BRANCH, SSD_XBC, SSD_HEADS, BRANCH, BRANCH, BRANCH, BRANCH, BRANCH, BRANCH, BRANCH,
            FOX_HEADS, 3 * D_MODEL)

LANES = 128
FF_TILE = 512
FF_SUB = 256
D_FF_PAD = 5632
VMEM_LIMIT = 56 * 1024 * 1024

_IN_OFFS = [sum(IN_SIZES[:i]) for i in range(len(IN_SIZES) + 1)]
DT_LANE0 = _IN_OFFS[2] % LANES
FOX_LANE0 = _IN_OFFS[10] % LANES
assert DT_LANE0 == 0 and FOX_LANE0 + FOX_HEADS <= LANES


def _params(*sem):
    return pltpu.CompilerParams(dimension_semantics=sem, vmem_limit_bytes=VMEM_LIMIT)


def _silu(x):
    return x * jax.nn.sigmoid(x)


def _log_sigmoid(x):
    return jnp.minimum(x, 0.0) - jnp.log(1.0 + jnp.exp(-jnp.abs(x)))


def _dot(a, b, **kw):
    return jnp.dot(a, b, preferred_element_type=F32, **kw)


def _dot_nt(a, b):
    return lax.dot_general(a, b, (((1,), (1,)), ((), ())), preferred_element_type=F32)


def _dot_tn(a, b):
    return lax.dot_general(a, b, (((0,), (0,)), ((), ())), preferred_element_type=F32)


def _tril(n):
    r = lax.broadcasted_iota(jnp.int32, (n, n), 0)
    c = lax.broadcasted_iota(jnp.int32, (n, n), 1)
    return (r >= c).astype(F32)


def _rmsnorm_body(x_ref, w_ref, o_ref):
    x = x_ref[...]
    ms = jnp.mean(x * x, axis=-1, keepdims=True)
    o_ref[...] = (x * lax.rsqrt(ms + NORM_EPS) * w_ref[...]).astype(o_ref.dtype)


def _rmsnorm(x, w, out_dtype):
    m, d = x.shape
    tm = min(512, m)
    return pl.pallas_call(
        _rmsnorm_body,
        grid=(m // tm,),
        in_specs=[pl.BlockSpec((tm, d), lambda i: (i, 0)), pl.BlockSpec((1, d), lambda i: (0, 0))],
        out_specs=pl.BlockSpec((tm, d), lambda i: (i, 0)),
        out_shape=jax.ShapeDtypeStruct((m, d), out_dtype),
        compiler_params=_params("parallel"),
        name="rmsnorm",
    )(x, w.reshape(1, d))


def _mm_body(a_ref, b_ref, o_ref):
    o_ref[...] = _dot(a_ref[...], b_ref[...]).astype(o_ref.dtype)


def _mm_nt_body(a_ref, b_ref, o_ref):
    o_ref[...] = _dot_nt(a_ref[...], b_ref[...]).astype(o_ref.dtype)


def _out_proj_body(a_ref, b_ref, r_ref, nw_ref, o_ref, h_ref):
    x = r_ref[...] + _dot(a_ref[...], b_ref[...])
    o_ref[...] = x
    ms = jnp.mean(x * x, axis=-1, keepdims=True)
    h_ref[...] = (x * lax.rsqrt(ms + NORM_EPS) * nw_ref[...]).astype(h_ref.dtype)


def _out_proj(a, w, x, norm_w):
    m, k = a.shape
    d = w.shape[1]
    tm = min(512, m)
    row = lambda i: (i, 0)
    fixed = lambda i: (0, 0)
    return pl.pallas_call(
        _out_proj_body,
        grid=(m // tm,),
        in_specs=[pl.BlockSpec((tm, k), row), pl.BlockSpec((k, d), fixed, pipeline_mode=pl.Buffered(1)),
                  pl.BlockSpec((tm, d), row), pl.BlockSpec((1, d), fixed)],
        out_specs=[pl.BlockSpec((tm, d), row), pl.BlockSpec((tm, d), row)],
        out_shape=[jax.ShapeDtypeStruct((m, d), F32), jax.ShapeDtypeStruct((m, d), BF16)],
        compiler_params=_params("parallel"),
        name="out_proj",
    )(a, w, x, norm_w.reshape(1, d))


def _mm_t_body(w_ref, a_ref, o_ref):
    o_ref[...] = _dot_nt(w_ref[...], a_ref[...]).astype(o_ref.dtype)


def _mm_t(w, a, out_dtype, name, tm=1024):
    n, k = w.shape
    m = a.shape[0]
    tm = min(tm, m)
    return pl.pallas_call(
        _mm_t_body,
        grid=(m // tm,),
        in_specs=[pl.BlockSpec((n, k), lambda i: (0, 0), pipeline_mode=pl.Buffered(1)),
                  pl.BlockSpec((tm, k), lambda i: (i, 0))],
        out_specs=pl.BlockSpec((n, tm), lambda i: (0, i)),
        out_shape=jax.ShapeDtypeStruct((n, m), out_dtype),
        compiler_params=_params("parallel"),
        name=name,
    )(w, a)


def _mm(a, b, out_dtype, tn, name, b_is_nk=False, tm=1024):
    m, k = a.shape
    n = b.shape[0] if b_is_nk else b.shape[1]
    tm = min(tm, m)
    mode = dict(pipeline_mode=pl.Buffered(1)) if n == tn else {}
    b_spec = (pl.BlockSpec((tn, k), lambda i, j: (j, 0), **mode) if b_is_nk
              else pl.BlockSpec((k, tn), lambda i, j: (0, j), **mode))
    in_specs = [pl.BlockSpec((tm, k), lambda i, j: (i, 0)), b_spec]
    args = [a, b]
    body = _mm_nt_body if b_is_nk else _mm_body
    return pl.pallas_call(
        body,
        grid=(m // tm, n // tn),
        in_specs=in_specs,
        out_specs=pl.BlockSpec((tm, tn), lambda i, j: (i, j)),
        out_shape=jax.ShapeDtypeStruct((m, n), out_dtype),
        compiler_params=_params("parallel", "parallel"),
        name=name,
    )(*args)


def _ssd_body(z_ref, xbc_ref, sm_ref, cw_ref, cb_ref, dtb_ref, alog_ref, d_ref, nw_ref, o_ref,
              halo_ref, s_ref, *, tt):
    @pl.when(pl.program_id(1) == 0)
    def _():
        halo_ref[...] = jnp.zeros_like(halo_ref)
        s_ref[...] = jnp.zeros_like(s_ref)

    xbc = xbc_ref[...]
    ext = jnp.concatenate([halo_ref[...], xbc], axis=0)
    halo_ref[...] = xbc[tt - 8:, :]
    cw = cw_ref[...]
    conv = cb_ref[...] + cw[SSD_CONV - 1:SSD_CONV] * xbc
    for k in range(SSD_CONV - 1):
        back = SSD_CONV - 1 - k
        conv = conv + cw[k:k + 1] * ext[8 - back:8 - back + tt]
    act = _silu(conv)
    xs_all = act[:, :BRANCH]
    dt_all = jax.nn.softplus(sm_ref[...] + dtb_ref[...])
    adt_all = dt_all * (-jnp.exp(alog_ref[...]))

    ln = SSD_CHUNK
    tril = _tril(ln)
    row = lax.broadcasted_iota(jnp.int32, (ln, ln), 0)
    col = lax.broadcasted_iota(jnp.int32, (ln, ln), 1)
    causal = row >= col
    lane = lax.broadcasted_iota(jnp.int32, (ln, LANES), 1)
    eh = lax.broadcasted_iota(jnp.int32, (LANES, BRANCH), 0)
    ec = lax.broadcasted_iota(jnp.int32, (LANES, BRANCH), 1)
    expand = (ec // SSD_HEAD_DIM == eh).astype(F32)
    gw = BRANCH // SSD_GROUPS

    for c in range(tt // ln):
        rs = slice(c * ln, (c + 1) * ln)
        xs = xs_all[rs]
        cs = _dot(tril, adt_all[rs], precision=HIGHEST)
        cs_t = cs.T
        cs_x = _dot(cs, expand, precision=HIGHEST)
        dt_x = _dot(dt_all[rs], expand, precision=HIGHEST)
        xdt = xs * dt_x
        last_x = cs_x[ln - 1:ln]
        xdt_st = (xdt * jnp.exp(last_x - cs_x)).astype(BF16)
        out_decay = jnp.exp(cs_x)
        chunk_decay = jnp.exp(last_x)
        xdt_b = xdt.astype(BF16)
        ys = []
        for g in range(SSD_GROUPS):
            b_g = act[rs, BRANCH + g * SSD_STATE:BRANCH + (g + 1) * SSD_STATE].astype(BF16)
            c_g = act[rs, BRANCH + (SSD_GROUPS + g) * SSD_STATE:
                      BRANCH + (SSD_GROUPS + g + 1) * SSD_STATE].astype(BF16)
            gs = slice(g * gw, (g + 1) * gw)
            cb = _dot_nt(c_g, b_g)
            state = s_ref[g]
            y_off = _dot(c_g, state.astype(BF16)) * out_decay[:, gs]
            s_ref[g] = state * chunk_decay[:, gs] + _dot_tn(b_g, xdt_st[:, gs])
            heads_per_group = SSD_HEADS // SSD_GROUPS
            pieces = []
            for p in range(heads_per_group // 2):
                h0 = g * heads_per_group + 2 * p
                xp = xdt_b[:, h0 * SSD_HEAD_DIM:(h0 + 2) * SSD_HEAD_DIM]
                res = []
                for h in (h0, h0 + 1):
                    seg = jnp.where(causal, cs[:, h:h + 1] - cs_t[h:h + 1, :], -jnp.inf)
                    res.append(_dot((cb * jnp.exp(seg)).astype(BF16), xp))
                pieces.append(jnp.where(lane < SSD_HEAD_DIM, res[0], res[1]))
            ys.append(jnp.concatenate(pieces, axis=1) + y_off)
        y = jnp.concatenate(ys, axis=1) + xs * d_ref[...]
        y = y * _silu(z_ref[rs, :])
        normed = []
        for g in range(SSD_GROUPS):
            yg = y[:, g * gw:(g + 1) * gw]
            ms = jnp.mean(yg * yg, axis=-1, keepdims=True)
            normed.append(yg * lax.rsqrt(ms + NORM_EPS))
        o_ref[rs, :] = (jnp.concatenate(normed, axis=1) * nw_ref[...]).astype(o_ref.dtype)


def _ssd(z, xbc, small, conv_w, conv_b, dt_bias, a_log, d_skip, norm_w, bsz, t):
    tt = min(512, t)
    nt = t // tt
    row = lambda b, i: (b * nt + i, 0)
    fixed = lambda b, i: (0, 0)
    pad = lambda v: jnp.pad(v, (0, LANES - SSD_HEADS)).reshape(1, LANES)
    return pl.pallas_call(
        functools.partial(_ssd_body, tt=tt),
        grid=(bsz, nt),
        in_specs=[
            pl.BlockSpec((tt, BRANCH), row),
            pl.BlockSpec((tt, SSD_XBC), row),
            pl.BlockSpec((tt, LANES), row),
            pl.BlockSpec((SSD_CONV, SSD_XBC), fixed),
            pl.BlockSpec((1, SSD_XBC), fixed),
            pl.BlockSpec((1, LANES), fixed),
            pl.BlockSpec((1, LANES), fixed),
            pl.BlockSpec((1, BRANCH), fixed),
            pl.BlockSpec((1, BRANCH), fixed),
        ],
        out_specs=pl.BlockSpec((tt, BRANCH), row),
        out_shape=jax.ShapeDtypeStruct((bsz * t, BRANCH), BF16),
        scratch_shapes=[pltpu.VMEM((8, SSD_XBC), F32),
                        pltpu.VMEM((SSD_GROUPS, SSD_STATE, BRANCH // SSD_GROUPS), F32)],
        compiler_params=_params("parallel", "arbitrary"),
        name="ssd",
    )(z, xbc, small, conv_w, conv_b.reshape(1, -1), pad(dt_bias), pad(a_log),
      jnp.repeat(d_skip, SSD_HEAD_DIM).reshape(1, -1), norm_w.reshape(1, -1))


def _hgrn_body(x_ref, loglb_ref, log1m_ref, onem_ref, nw_ref, o_ref, st_ref, *, tt):
    @pl.when(pl.program_id(1) == 0)
    def _():
        st_ref[...] = jnp.zeros_like(st_ref)

    cl, sub = HGRN_CHUNK, HGRN_SUB
    nsub = cl // sub
    tril = _tril(cl)
    t_idx = lax.broadcasted_iota(jnp.int32, (sub, HGRN_DK), 0)
    lane = lax.broadcasted_iota(jnp.int32, (sub, cl), 1)

    def chunk(c, carry):
        r0 = pl.multiple_of(c * cl, cl)
        hq = x_ref[pl.ds(r0, cl), 0 * BRANCH:1 * BRANCH]
        hf = x_ref[pl.ds(r0, cl), 1 * BRANCH:2 * BRANCH]
        hi = x_ref[pl.ds(r0, cl), 2 * BRANCH:3 * BRANCH]
        hg = x_ref[pl.ds(r0, cl), 3 * BRANCH:4 * BRANCH]
        e = jnp.exp(-jnp.abs(hf))
        d = 1.0 + e
        kk = onem_ref[...] * (jnp.where(hf >= 0.0, e, 1.0) / d)
        la = loglb_ref[...]
        lb2 = log1m_ref[...] + (jnp.minimum(hf, 0.0) - jnp.log(d))
        log_f = jnp.maximum(la, lb2) + jnp.log(1.0 + jnp.exp(-jnp.abs(la - lb2)))
        q = _silu(hq)
        b = _dot(tril, log_f * LOG2E, precision=HIGHEST)
        b_last = b[cl - 1:cl]
        q_in = (q * jnp.exp2(b)).astype(BF16)
        k_st = (kk * jnp.exp2(b_last - b)).astype(BF16)
        dec = jnp.exp2(b_last)
        v_b = hi.astype(BF16)
        outs = []
        for h in range(HGRN_HEADS):
            sl = slice(h * HGRN_DK, (h + 1) * HGRN_DK)
            bh, qh, kh, vh = b[:, sl], q[:, sl], kk[:, sl], v_b[:, sl]
            st = st_ref[h]
            o_h = _dot_nt(q_in[:, sl], st.astype(BF16))
            st_ref[h] = st * dec[:, sl] + _dot_tn(vh, k_st[:, sl])
            a_rows = []
            for i in range(nsub):
                i0 = i * sub
                bi, qi = bh[i0:i0 + sub], qh[i0:i0 + sub]
                lhs, rhs = [], []
                if i > 0:
                    beta = bh[i0 - 1:i0]
                    lhs.append(qi * jnp.exp2(bi - beta))
                    rhs.append(kh[:i0] * jnp.exp2(beta - bh[:i0]))
                for s in range(sub):
                    lhs.append(qi * jnp.exp2(jnp.where(t_idx >= s, bi - bi[s:s + 1], -jnp.inf)))
                rhs.append(kh[i0:i0 + sub])
                if i0 + sub < cl:
                    rhs.append(jnp.zeros((cl - i0 - sub, HGRN_DK), F32))
                res = _dot_nt(jnp.concatenate(lhs, axis=0).astype(BF16),
                              jnp.concatenate(rhs, axis=0).astype(BF16))
                off = sub if i > 0 else 0
                a_i = jnp.where(lane < i0, res[:sub], 0.0) if i > 0 else jnp.zeros((sub, cl), F32)
                for s in range(sub):
                    a_i = a_i + jnp.where(lane == i0 + s, res[off + s * sub:off + (s + 1) * sub], 0.0)
                a_rows.append(a_i)
            o_h = o_h + _dot(jnp.concatenate(a_rows, axis=0).astype(BF16), vh)
            ms = jnp.mean(o_h * o_h, axis=-1, keepdims=True)
            outs.append(o_h * lax.rsqrt(ms + NORM_EPS))
        o = jnp.concatenate(outs, axis=1) * nw_ref[...]
        o_ref[pl.ds(r0, cl), :] = (o * _silu(hg)).astype(o_ref.dtype)
        return carry

    lax.fori_loop(0, tt // cl, chunk, 0)


def _hgrn(proj, lb, norm_w, bsz, t):
    tt = min(256, t)
    nt = t // tt
    row = lambda b, i: (b * nt + i, 0)
    fixed = lambda b, i: (0, 0)
    lb = lb.reshape(1, BRANCH)
    return pl.pallas_call(
        functools.partial(_hgrn_body, tt=tt),
        grid=(bsz, nt),
        in_specs=[pl.BlockSpec((tt, 4 * BRANCH), row)] + [pl.BlockSpec((1, BRANCH), fixed)] * 4,
        out_specs=pl.BlockSpec((tt, BRANCH), row),
        out_shape=jax.ShapeDtypeStruct((bsz * t, BRANCH), BF16),
        scratch_shapes=[pltpu.VMEM((HGRN_HEADS, HGRN_DK, HGRN_DK), F32)],
        compiler_params=_params("parallel", "arbitrary"),
        name="hgrn",
    )(proj, jnp.log(lb), jnp.log1p(-lb), 1.0 - lb, norm_w.reshape(1, BRANCH))


FOX_BIAS_STRIDE = 8


def _fox_bias_tables():
    sel_q = np.zeros((3 * LANES, BRANCH), np.float32)
    sel_k = np.zeros((3 * LANES, BRANCH), np.float32)
    one_q = np.zeros((1, BRANCH), np.float32)
    one_k = np.zeros((1, BRANCH), np.float32)
    for h in range(FOX_HEADS):
        base = (h // 2) * LANES + FOX_BIAS_STRIDE * (h % 2)
        for j in range(3):
            sel_q[j * LANES + FOX_LANE0 + h, base + j] = 1.0
            sel_k[j * LANES + FOX_LANE0 + h, base + 3 + j] = -1.0
            one_q[0, base + 3 + j] = 1.0
            one_k[0, base + j] = 1.0
    return jnp.asarray(sel_q, BF16), jnp.asarray(sel_k, BF16), jnp.asarray(one_q), jnp.asarray(one_k)


def _fox_prep_body(sm_ref, bias_ref, selq_ref, selk_ref, oneq_ref, onek_ref, qb_ref, kb_ref, pieces_ref, *, t):
    ln = LANES
    tril = _tril(ln)
    carry = jnp.zeros((1, LANES), F32)
    for c in range(t // ln):
        rs = slice(c * ln, (c + 1) * ln)
        lf = _log_sigmoid(sm_ref[rs, :] + bias_ref[...])
        cs = _dot(tril, lf, precision=HIGHEST) + carry
        carry = cs[ln - 1:ln]
        hi = cs.astype(BF16)
        rest = cs - hi.astype(F32)
        mid = rest.astype(BF16)
        lo = (rest - mid.astype(F32)).astype(BF16)
        pieces_ref[rs, :] = jnp.concatenate([hi, mid, lo], axis=1)
    pieces = pieces_ref[...]
    qb_ref[...] = (_dot(pieces, selq_ref[...]) + oneq_ref[...]).astype(BF16)
    kb_ref[...] = (_dot(pieces, selk_ref[...]) + onek_ref[...]).astype(BF16)


def _fox_prep(small, f_bias, bsz, t):
    bias = jnp.pad(f_bias, (FOX_LANE0, LANES - FOX_LANE0 - FOX_HEADS)).reshape(1, LANES)
    fixed = lambda b: (0, 0)
    out = jax.ShapeDtypeStruct((bsz * t, BRANCH), BF16)
    return pl.pallas_call(
        functools.partial(_fox_prep_body, t=t),
        grid=(bsz,),
        in_specs=[pl.BlockSpec((t, LANES), lambda b: (b, 1)), pl.BlockSpec((1, LANES), fixed),
                  pl.BlockSpec((3 * LANES, BRANCH), fixed), pl.BlockSpec((3 * LANES, BRANCH), fixed),
                  pl.BlockSpec((1, BRANCH), fixed), pl.BlockSpec((1, BRANCH), fixed)],
        out_specs=[pl.BlockSpec((t, BRANCH), lambda b: (b, 0))] * 2,
        out_shape=[out, out],
        scratch_shapes=[pltpu.VMEM((t, 3 * LANES), BF16)],
        compiler_params=_params("parallel"),
        name="fox_prep",
    )(small, bias, *_fox_bias_tables())


FOX_ONES_ROWS = 16


FOX_PAIRS_PER_STEP = 2


def _fox_body(q_ref, k_ref, vt_in_ref, qb_ref, kb_ref, o_ref, *scratch, tq, t):
    qi = pl.program_id(2)
    hd = FOX_HEAD_DIM
    lane = lax.broadcasted_iota(jnp.int32, (tq, LANES), 1)
    sub = lax.broadcasted_iota(jnp.int32, (LANES, tq), 0)
    key = lax.broadcasted_iota(jnp.int32, (tq, 2 * tq), 0)
    qry = lax.broadcasted_iota(jnp.int32, (tq, 2 * tq), 1) % tq
    pair_scratch = [scratch[4 * r:4 * r + 4] for r in range(FOX_PAIRS_PER_STEP)]

    @pl.when(qi == 0)
    def _():
        for r, (kaug_ref, vt_ref, _, _) in enumerate(pair_scratch):
            ls = slice(r * LANES, (r + 1) * LANES)
            kaug_ref[:, :LANES] = k_ref[:, ls]
            kaug_ref[:, LANES:] = kb_ref[:, ls]
            vt_ref[:LANES, :] = vt_in_ref[ls, :]
            vt_ref[LANES:, :] = jnp.ones((FOX_ONES_ROWS, t), BF16)

    q_augs = []
    for r in range(FOX_PAIRS_PER_STEP):
        ls = slice(r * LANES, (r + 1) * LANES)
        q = q_ref[:, ls] * (hd ** -0.5)
        qb = qb_ref[:, ls]
        rows = []
        for h in range(2):
            mine = (lane < hd) if h == 0 else (lane >= hd)
            mine_b = (lane < FOX_BIAS_STRIDE) if h == 0 else (lane >= FOX_BIAS_STRIDE)
            rows.append(jnp.concatenate([jnp.where(mine, q, jnp.zeros_like(q)),
                                         jnp.where(mine_b, qb, jnp.zeros_like(qb))], axis=1))
        q_augs.append(jnp.concatenate(rows, axis=0))

    for c in range(t // tq):
        @pl.when(qi == c)
        def _(c=c):
            nk = (c + 1) * tq
            for r, (kaug_ref, vt_ref, s_ref, p_ref) in enumerate(pair_scratch):
                s_ref[:nk, :] = _dot_nt(kaug_ref[:nk, :], q_augs[r])
            for r, (kaug_ref, vt_ref, s_ref, p_ref) in enumerate(pair_scratch):
                diag = jnp.where(key <= qry, s_ref[c * tq:nk, :], -jnp.inf)
                m = jnp.max(diag, axis=0, keepdims=True)
                if c > 0:
                    m = jnp.maximum(m, jnp.max(s_ref[:c * tq, :], axis=0, keepdims=True))
                    p_ref[:c * tq, :] = jnp.exp(s_ref[:c * tq, :] - m).astype(BF16)
                p_ref[c * tq:nk, :] = jnp.exp(diag - m).astype(BF16)
            for r, (kaug_ref, vt_ref, s_ref, p_ref) in enumerate(pair_scratch):
                acc = _dot(vt_ref[:, :nk], p_ref[:nk, :])
                o0 = acc[:LANES, :tq] / acc[LANES:LANES + 1, :tq]
                o1 = acc[:LANES, tq:] / acc[LANES:LANES + 1, tq:]
                o_ref[:, r * LANES:(r + 1) * LANES] = jnp.where(sub < hd, o0, o1).T.astype(o_ref.dtype)


def _fox(qk, vt, qbias, kbias, bsz, t):
    tq = min(256, t)
    nq = t // tq
    w = FOX_PAIRS_PER_STEP * LANES
    groups = BRANCH // w
    per_pair = [pltpu.VMEM((t, 2 * LANES), BF16), pltpu.VMEM((LANES + FOX_ONES_ROWS, t), BF16),
                pltpu.VMEM((t, 2 * tq), F32), pltpu.VMEM((t, 2 * tq), BF16)]
    return pl.pallas_call(
        functools.partial(_fox_body, tq=tq, t=t),
        grid=(bsz, groups, nq),
        in_specs=[
            pl.BlockSpec((tq, w), lambda b, g, i: (b * nq + i, g)),
            pl.BlockSpec((t, w), lambda b, g, i: (b, groups + g)),
            pl.BlockSpec((w, t), lambda b, g, i: (g, b)),
            pl.BlockSpec((tq, w), lambda b, g, i: (b * nq + i, g)),
            pl.BlockSpec((t, w), lambda b, g, i: (b, g)),
        ],
        out_specs=pl.BlockSpec((tq, w), lambda b, g, i: (b * nq + i, g)),
        out_shape=jax.ShapeDtypeStruct((bsz * t, BRANCH), BF16),
        scratch_shapes=per_pair * FOX_PAIRS_PER_STEP,
        compiler_params=_params("parallel", "parallel", "arbitrary"),
        name="fox",
    )(qk, qk, vt, qbias, kbias)


def _merge_body(y0_ref, y1_ref, y2_ref, w0_ref, w1_ref, w2_ref, g0_ref, g1_ref, g2_ref, o_ref):
    acc = jax.nn.sigmoid(g0_ref[...]) * _dot(y0_ref[...], w0_ref[...])
    acc = acc + jax.nn.sigmoid(g1_ref[...]) * _dot(y1_ref[...], w1_ref[...])
    acc = acc + jax.nn.sigmoid(g2_ref[...]) * _dot(y2_ref[...], w2_ref[...])
    o_ref[...] = acc.astype(o_ref.dtype)


def _merge(ys, ws, gates):
    m = gates.shape[0]
    tm = min(512, m)
    y_spec = pl.BlockSpec((tm, BRANCH), lambda i: (i, 0))
    w_spec = pl.BlockSpec((BRANCH, D_MODEL), lambda i: (0, 0), pipeline_mode=pl.Buffered(1))
    g_specs = [pl.BlockSpec((tm, D_MODEL), functools.partial(lambda i, r: (i, r), r=r)) for r in range(3)]
    return pl.pallas_call(
        _merge_body,
        grid=(m // tm,),
        in_specs=[y_spec] * 3 + [w_spec] * 3 + g_specs,
        out_specs=pl.BlockSpec((tm, D_MODEL), lambda i: (i, 0)),
        out_shape=jax.ShapeDtypeStruct((m, D_MODEL), BF16),
        compiler_params=_params("parallel"),
        name="merge",
    )(*ys, *ws, gates, gates, gates)


def _ffn_body(h_ref, halo_ref, x_ref, nw_ref, wg_ref, wu_ref, cwg_ref, cwu_ref, cbg_ref, cbu_ref, wd_ref,
              *rest, tm, tiles_per_seq, emit_x):
    if emit_x:
        o_ref, n_ref, hh_ref, acc_ref, ug_ref, uu_ref = rest
    else:
        n_ref, hh_ref, acc_ref, ug_ref, uu_ref = rest
    i = pl.program_id(0)
    j = pl.program_id(1)
    pad = 16

    @pl.when(j == 0)
    def _():
        first = (i % tiles_per_seq) == 0
        halo = halo_ref[...]
        hh_ref[:pad, :] = jnp.where(first, jnp.zeros_like(halo), halo)
        hh_ref[pad:, :] = h_ref[...]
        acc_ref[...] = x_ref[...]

    hh = hh_ref[...]
    ug_ref[...] = _dot(hh, wg_ref[...])
    uu_ref[...] = _dot(hh, wu_ref[...])

    def conv(u_ref, cw_ref, cb_ref, cs):
        cw = cw_ref[:, cs]
        out = cb_ref[:, cs] + cw[FFN_CONV - 1:FFN_CONV] * u_ref[pad:, cs]
        for k in range(FFN_CONV - 1):
            back = FFN_CONV - 1 - k
            out = out + cw[k:k + 1] * u_ref[pad - back:pad - back + tm, cs]
        return out

    down = None
    for s0 in range(0, FF_TILE, FF_SUB):
        cs = slice(s0, s0 + FF_SUB)
        act = (_silu(conv(ug_ref, cwg_ref, cbg_ref, cs)) * conv(uu_ref, cwu_ref, cbu_ref, cs)).astype(BF16)
        part = _dot(act, wd_ref[cs, :])
        down = part if down is None else down + part
    acc_ref[...] += down

    @pl.when(j == pl.num_programs(1) - 1)
    def _():
        x_new = acc_ref[...]
        if emit_x:
            o_ref[...] = x_new
        ms = jnp.mean(x_new * x_new, axis=-1, keepdims=True)
        n_ref[...] = (x_new * lax.rsqrt(ms + NORM_EPS) * nw_ref[...]).astype(n_ref.dtype)


def _ffn(h, x, next_norm_w, wg, wu, cwg, cwu, cbg, cbu, wd, t, emit_x, norm_dtype):
    m = x.shape[0]
    tm = min(512, t)
    nf = D_FF_PAD // FF_TILE
    hb = tm // 16
    col = lambda i, j: (0, j)
    row = lambda i, j: (i, 0)
    out_specs = [pl.BlockSpec((tm, D_MODEL), row)]
    out_shape = [jax.ShapeDtypeStruct((m, D_MODEL), norm_dtype)]
    if emit_x:
        out_specs = [pl.BlockSpec((tm, D_MODEL), row)] + out_specs
        out_shape = [jax.ShapeDtypeStruct((m, D_MODEL), F32)] + out_shape
    return pl.pallas_call(
        functools.partial(_ffn_body, tm=tm, tiles_per_seq=t // tm, emit_x=emit_x),
        grid=(m // tm, nf),
        in_specs=[
            pl.BlockSpec((tm, D_MODEL), row),
            pl.BlockSpec((16, D_MODEL), lambda i, j: (jnp.maximum(i * hb - 1, 0), 0)),
            pl.BlockSpec((tm, D_MODEL), row),
            pl.BlockSpec((1, D_MODEL), lambda i, j: (0, 0)),
            pl.BlockSpec((D_MODEL, FF_TILE), col),
            pl.BlockSpec((D_MODEL, FF_TILE), col),
            pl.BlockSpec((FFN_CONV, FF_TILE), col),
            pl.BlockSpec((FFN_CONV, FF_TILE), col),
            pl.BlockSpec((1, FF_TILE), col),
            pl.BlockSpec((1, FF_TILE), col),
            pl.BlockSpec((FF_TILE, D_MODEL), lambda i, j: (j, 0)),
        ],
        out_specs=out_specs,
        out_shape=out_shape,
        scratch_shapes=[pltpu.VMEM((16 + tm, D_MODEL), BF16), pltpu.VMEM((tm, D_MODEL), F32),
                        pltpu.VMEM((16 + tm, FF_TILE), F32), pltpu.VMEM((16 + tm, FF_TILE), F32)],
        compiler_params=_params("parallel", "arbitrary"),
        name="ffn",
    )(h, h, x, next_norm_w.reshape(1, D_MODEL), wg, wu, cwg, cwu, cbg, cbu, wd)


def _split_up_body(w_ref, g_ref, u_ref):
    zeros = jnp.zeros((g_ref.shape[0], D_FF_PAD - D_FF), BF16)
    g_ref[:, :D_FF] = w_ref[:, :D_FF].astype(BF16)
    g_ref[:, D_FF:] = zeros
    u_ref[:, :D_FF] = w_ref[:, D_FF:].astype(BF16)
    u_ref[:, D_FF:] = zeros


def _split_up_weights(w_up_all, l):
    tr = 256
    out = jax.ShapeDtypeStruct((D_MODEL, D_FF_PAD), BF16)
    return pl.pallas_call(
        _split_up_body,
        grid=(D_MODEL // tr,),
        in_specs=[pl.BlockSpec((None, tr, 2 * D_FF), lambda i: (l, i, 0))],
        out_specs=[pl.BlockSpec((tr, D_FF_PAD), lambda i: (i, 0))] * 2,
        out_shape=[out, out],
        compiler_params=_params("parallel"),
        name="split_up_weights",
    )(w_up_all)


def _ffn_weights(w_up_all, l, conv_w, conv_b, w_down):
    padc = lambda a: jnp.pad(a, [(0, 0)] * (a.ndim - 1) + [(0, D_FF_PAD - D_FF)])
    halves = lambda a: (padc(a[..., :D_FF]), padc(a[..., D_FF:]))
    wg, wu = _split_up_weights(w_up_all, l)
    cwg, cwu = halves(conv_w)
    cbg, cbu = halves(conv_b.reshape(1, -1))
    wd = jnp.pad(w_down, ((0, D_FF_PAD - D_FF), (0, 0))).astype(BF16)
    return wg, wu, cwg, cwu, cbg, cbu, wd


def _in_proj_weights(w):
    wt = w.T
    offs = _IN_OFFS
    seg = lambda a, b: wt[offs[a]:offs[b]]
    window = lambda i: wt[offs[i] - offs[i] % LANES:offs[i] - offs[i] % LANES + LANES]
    small = jnp.concatenate([window(2), window(10)], axis=0)
    cast = lambda a: a.astype(BF16)
    return dict(z=cast(seg(0, 1)), xbc=cast(seg(1, 2)), hgrn=cast(seg(3, 7)), fox_qk=cast(seg(7, 9)),
                fox_v=cast(seg(9, 10)), gates=cast(seg(11, 12)), small=cast(small))


def kernel(x, norm_mix_w, w_in, ssd_conv_w, ssd_conv_b, ssd_dt_bias, ssd_a_log, ssd_d, ssd_norm_w, hgrn_lb, hgrn_norm_w, fox_f_bias, w_branch_ssd, w_branch_hgrn, w_branch_fox, w_out, norm_ffn_w, ffn_w_up, ffn_conv_w, ffn_conv_b, ffn_w_down, final_norm_w):
    bsz, t, d = x.shape
    depth = w_in.shape[0]
    lbs = jnp.cumsum(jax.nn.softmax(hgrn_lb.astype(F32), axis=0), axis=0)
    lbs = lbs - lbs[0]
    x = x.reshape(bsz * t, d)
    h = _rmsnorm(x, norm_mix_w[0], BF16)
    for l in range(depth):
        wi = _in_proj_weights(w_in[l])
        proj = lambda key, dtype, tn: _mm(h, wi[key], dtype, tn, "proj_" + key, b_is_nk=True)
        z = proj("z", F32, 1024)
        xbc = proj("xbc", F32, SSD_XBC)
        small = proj("small", F32, 2 * LANES)
        p_hgrn = proj("hgrn", F32, 1024)
        fox_qk = proj("fox_qk", BF16, 1024)
        fox_vt = _mm_t(wi["fox_v"], h, BF16, "proj_fox_vt")
        gates = proj("gates", F32, 1024)
        y_ssd = _ssd(z, xbc, small, ssd_conv_w[l], ssd_conv_b[l], ssd_dt_bias[l], ssd_a_log[l], ssd_d[l],
                     ssd_norm_w[l], bsz, t)
        y_hgrn = _hgrn(p_hgrn, lbs[l], hgrn_norm_w[l], bsz, t)
        y_fox = _fox(fox_qk, fox_vt, *_fox_prep(small, fox_f_bias[l], bsz, t), bsz, t)
        merged = _merge((y_ssd, y_hgrn, y_fox),
                        (w_branch_ssd[l].astype(BF16), w_branch_hgrn[l].astype(BF16), w_branch_fox[l].astype(BF16)),
                        gates)
        x, h = _out_proj(merged, w_out[l].astype(BF16), x, norm_ffn_w[l])
        ffn_w = _ffn_weights(ffn_w_up, l, ffn_conv_w[l], ffn_conv_b[l], ffn_w_down[l])
        if l + 1 < depth:
            x, h = _ffn(h, x, norm_mix_w[l + 1], *ffn_w, t, True, BF16)
        else:
            (out,) = _ffn(h, x, final_norm_w, *ffn_w, t, False, F32)
    return out.reshape(bsz, t, d)
```

```python
import functools

import jax
import jax.numpy as jnp
import numpy as np
from jax import lax
from jax.experimental import pallas as pl
from jax.experimental.pallas import tpu as pltpu

F32 = jnp.float32
BF16 = jnp.bfloat16
LOG2E = 1.4426950408889634

D_MODEL = 2048
BRANCH = 1024
SSD_HEADS = 16
SSD_HEAD_DIM = 64
SSD_GROUPS = 2
SSD_STATE = 128
SSD_CONV = 4
SSD_CHUNK = 128
SSD_XBC = BRANCH + 2 * SSD_GROUPS * SSD_STATE
HGRN_HEADS = 8
HGRN_DK = 128
HGRN_CHUNK = 64
HGRN_SUB = 8
FOX_HEADS = 16
FOX_HEAD_DIM = 64
D_FF = 5504
FFN_CONV = 3
NORM_EPS = 1e-6
IN_SIZES = (BRANCH, SSD_XBC, SSD_HEADS, BRANCH, BRANCH, BRANCH, BRANCH, BRANCH, BRANCH, BRANCH,
            FOX_HEADS, 3 * D_MODEL)

LANES = 128
FF_TILE = 512
FF_SUB = 256
D_FF_PAD = 5632
VMEM_LIMIT = 56 * 1024 * 1024

_IN_OFFS = [sum(IN_SIZES[:i]) for i in range(len(IN_SIZES) + 1)]
DT_LANE0 = _IN_OFFS[2] % LANES
FOX_LANE0 = _IN_OFFS[10] % LANES
assert DT_LANE0 == 0 and FOX_LANE0 + FOX_HEADS <= LANES


def _params(*sem):
    return pltpu.CompilerParams(dimension_semantics=sem, vmem_limit_bytes=VMEM_LIMIT)


def _silu(x):
    return x * jax.nn.sigmoid(x)


def _log_sigmoid(x):
    return jnp.minimum(x, 0.0) - jnp.log(1.0 + jnp.exp(-jnp.abs(x)))


def _dot(a, b, **kw):
    return jnp.dot(a, b, preferred_element_type=F32, **kw)


def _dot_nt(a, b):
    return lax.dot_general(a, b, (((1,), (1,)), ((), ())), preferred_element_type=F32)


def _dot_tn(a, b):
    return lax.dot_general(a, b, (((0,), (0,)), ((), ())), preferred_element_type=F32)


def _tril(n):
    r = lax.broadcasted_iota(jnp.int32, (n, n), 0)
    c = lax.broadcasted_iota(jnp.int32, (n, n), 1)
    return (r >= c).astype(BF16)


def _split3(x):
    hi = x.astype(BF16)
    rest = x - hi.astype(F32)
    mid = rest.astype(BF16)
    lo = (rest - mid.astype(F32)).astype(BF16)
    return hi, mid, lo


def _select_dot(sel, x):
    return _dot(jnp.concatenate([sel, sel, sel], axis=1), jnp.concatenate(_split3(x), axis=0))


def _dot_select(x, sel):
    return _dot(jnp.concatenate(_split3(x), axis=1), jnp.concatenate([sel, sel, sel], axis=0))


def _rmsnorm_body(x_ref, w_ref, o_ref):
    x = x_ref[...]
    ms = jnp.mean(x * x, axis=-1, keepdims=True)
    o_ref[...] = (x * lax.rsqrt(ms + NORM_EPS) * w_ref[...]).astype(o_ref.dtype)


def _rmsnorm(x, w, out_dtype):
    m, d = x.shape
    tm = min(512, m)
    return pl.pallas_call(
        _rmsnorm_body,
        grid=(m // tm,),
        in_specs=[pl.BlockSpec((tm, d), lambda i: (i, 0)), pl.BlockSpec((1, d), lambda i: (0, 0))],
        out_specs=pl.BlockSpec((tm, d), lambda i: (i, 0)),
        out_shape=jax.ShapeDtypeStruct((m, d), out_dtype),
        compiler_params=_params("parallel"),
        name="rmsnorm",
    )(x, w.reshape(1, d))


def _mm_body(a_ref, b_ref, o_ref):
    o_ref[...] = _dot(a_ref[...], b_ref[...]).astype(o_ref.dtype)


def _mm_nt_body(a_ref, b_ref, o_ref):
    o_ref[...] = _dot_nt(a_ref[...], b_ref[...]).astype(o_ref.dtype)


def _out_proj_body(a_ref, b_ref, r_ref, nw_ref, o_ref, h_ref):
    x = r_ref[...] + _dot(a_ref[...], b_ref[...])
    o_ref[...] = x
    ms = jnp.mean(x * x, axis=-1, keepdims=True)
    h_ref[...] = (x * lax.rsqrt(ms + NORM_EPS) * nw_ref[...]).astype(h_ref.dtype)


def _out_proj(a, w, x, norm_w):
    m, k = a.shape
    d = w.shape[1]
    tm = min(512, m)
    row = lambda i: (i, 0)
    fixed = lambda i: (0, 0)
    return pl.pallas_call(
        _out_proj_body,
        grid=(m // tm,),
        in_specs=[pl.BlockSpec((tm, k), row), pl.BlockSpec((k, d), fixed, pipeline_mode=pl.Buffered(1)),
                  pl.BlockSpec((tm, d), row), pl.BlockSpec((1, d), fixed)],
        out_specs=[pl.BlockSpec((tm, d), row), pl.BlockSpec((tm, d), row)],
        out_shape=[jax.ShapeDtypeStruct((m, d), F32), jax.ShapeDtypeStruct((m, d), BF16)],
        compiler_params=_params("parallel"),
        name="out_proj",
    )(a, w, x, norm_w.reshape(1, d))


def _mm_t_body(w_ref, a_ref, o_ref):
    o_ref[...] = _dot_nt(w_ref[...], a_ref[...]).astype(o_ref.dtype)


def _mm_t(w, a, out_dtype, name, tm=1024):
    n, k = w.shape
    m = a.shape[0]
    tm = min(tm, m)
    return pl.pallas_call(
        _mm_t_body,
        grid=(m // tm,),
        in_specs=[pl.BlockSpec((n, k), lambda i: (0, 0), pipeline_mode=pl.Buffered(1)),
                  pl.BlockSpec((tm, k), lambda i: (i, 0))],
        out_specs=pl.BlockSpec((n, tm), lambda i: (0, i)),
        out_shape=jax.ShapeDtypeStruct((n, m), out_dtype),
        compiler_params=_params("parallel"),
        name=name,
    )(w, a)


def _mm(a, b, out_dtype, tn, name, b_is_nk=False, tm=1024):
    m, k = a.shape
    n = b.shape[0] if b_is_nk else b.shape[1]
    tm = min(tm, m)
    mode = dict(pipeline_mode=pl.Buffered(1)) if n == tn else {}
    b_spec = (pl.BlockSpec((tn, k), lambda i, j: (j, 0), **mode) if b_is_nk
              else pl.BlockSpec((k, tn), lambda i, j: (0, j), **mode))
    in_specs = [pl.BlockSpec((tm, k), lambda i, j: (i, 0)), b_spec]
    args = [a, b]
    body = _mm_nt_body if b_is_nk else _mm_body
    return pl.pallas_call(
        body,
        grid=(m // tm, n // tn),
        in_specs=in_specs,
        out_specs=pl.BlockSpec((tm, tn), lambda i, j: (i, j)),
        out_shape=jax.ShapeDtypeStruct((m, n), out_dtype),
        compiler_params=_params("parallel", "parallel"),
        name=name,
    )(*args)


def _ssd_body(z_ref, xbc_ref, sm_ref, cw_ref, cb_ref, dtb_ref, alog_ref, d_ref, nw_ref, o_ref,
              ext_ref, s_ref, *, tt):
    @pl.when(pl.program_id(1) == 0)
    def _():
        ext_ref[:8, :] = jnp.zeros((8, SSD_XBC), F32)
        s_ref[...] = jnp.zeros_like(s_ref)

    ext_ref[8:, :] = xbc_ref[...]
    cw = cw_ref[...]
    conv = cb_ref[...] + cw[SSD_CONV - 1:SSD_CONV] * xbc_ref[...]
    for k in range(SSD_CONV - 1):
        back = SSD_CONV - 1 - k
        conv = conv + cw[k:k + 1] * ext_ref[8 - back:8 - back + tt, :]
    ext_ref[:8, :] = xbc_ref[tt - 8:, :]
    act = _silu(conv)
    xs_all = act[:, :BRANCH]
    dt_all = jax.nn.softplus(sm_ref[...] + dtb_ref[...])
    adt_all = dt_all * (-jnp.exp(alog_ref[...]))

    ln = SSD_CHUNK
    tril = _tril(ln)
    row = lax.broadcasted_iota(jnp.int32, (ln, ln), 0)
    col = lax.broadcasted_iota(jnp.int32, (ln, ln), 1)
    causal = row >= col
    lane = lax.broadcasted_iota(jnp.int32, (ln, LANES), 1)
    eh = lax.broadcasted_iota(jnp.int32, (LANES, BRANCH), 0)
    ec = lax.broadcasted_iota(jnp.int32, (LANES, BRANCH), 1)
    expand = (ec // SSD_HEAD_DIM == eh).astype(BF16)
    gw = BRANCH // SSD_GROUPS

    for c in range(tt // ln):
        rs = slice(c * ln, (c + 1) * ln)
        xs = xs_all[rs]
        cs = _select_dot(tril, adt_all[rs])
        cs_t = cs.T
        cs_x = _dot_select(cs, expand)
        dt_x = _dot_select(dt_all[rs], expand)
        xdt = xs * dt_x
        last_x = cs_x[ln - 1:ln]
        xdt_st = (xdt * jnp.exp(last_x - cs_x)).astype(BF16)
        out_decay = jnp.exp(cs_x)
        chunk_decay = jnp.exp(last_x)
        xdt_b = xdt.astype(BF16)
        ys = []
        for g in range(SSD_GROUPS):
            b_g = act[rs, BRANCH + g * SSD_STATE:BRANCH + (g + 1) * SSD_STATE].astype(BF16)
            c_g = act[rs, BRANCH + (SSD_GROUPS + g) * SSD_STATE:
                      BRANCH + (SSD_GROUPS + g + 1) * SSD_STATE].astype(BF16)
            gs = slice(g * gw, (g + 1) * gw)
            cb = _dot_nt(c_g, b_g)
            state = s_ref[g]
            y_off = _dot(c_g, state.astype(BF16)) * out_decay[:, gs]
            s_ref[g] = state * chunk_decay[:, gs] + _dot_tn(b_g, xdt_st[:, gs])
            heads_per_group = SSD_HEADS // SSD_GROUPS
            pieces = []
            for p in range(heads_per_group // 2):
                h0 = g * heads_per_group + 2 * p
                xp = xdt_b[:, h0 * SSD_HEAD_DIM:(h0 + 2) * SSD_HEAD_DIM]
                res = []
                for h in (h0, h0 + 1):
                    seg = jnp.where(causal, cs[:, h:h + 1] - cs_t[h:h + 1, :], -jnp.inf)
                    res.append(_dot((cb * jnp.exp(seg)).astype(BF16), xp))
                pieces.append(jnp.where(lane < SSD_HEAD_DIM, res[0], res[1]))
            ys.append(jnp.concatenate(pieces, axis=1) + y_off)
        y = jnp.concatenate(ys, axis=1) + xs * d_ref[...]
        y = y * _silu(z_ref[rs, :])
        normed = []
        for g in range(SSD_GROUPS):
            yg = y[:, g * gw:(g + 1) * gw]
            ms = jnp.mean(yg * yg, axis=-1, keepdims=True)
            normed.append(yg * lax.rsqrt(ms + NORM_EPS))
        o_ref[rs, :] = (jnp.concatenate(normed, axis=1) * nw_ref[...]).astype(o_ref.dtype)


def _ssd(z, xbc, small, conv_w, conv_b, dt_bias, a_log, d_skip, norm_w, bsz, t):
    tt = min(512, t)
    nt = t // tt
    row = lambda b, i: (b * nt + i, 0)
    fixed = lambda b, i: (0, 0)
    pad = lambda v: jnp.pad(v, (0, LANES - SSD_HEADS)).reshape(1, LANES)
    return pl.pallas_call(
        functools.partial(_ssd_body, tt=tt),
        grid=(bsz, nt),
        in_specs=[
            pl.BlockSpec((tt, BRANCH), row),
            pl.BlockSpec((tt, SSD_XBC), row),
            pl.BlockSpec((tt, LANES), row),
            pl.BlockSpec((SSD_CONV, SSD_XBC), fixed),
            pl.BlockSpec((1, SSD_XBC), fixed),
            pl.BlockSpec((1, LANES), fixed),
            pl.BlockSpec((1, LANES), fixed),
            pl.BlockSpec((1, BRANCH), fixed),
            pl.BlockSpec((1, BRANCH), fixed),
        ],
        out_specs=pl.BlockSpec((tt, BRANCH), row),
        out_shape=jax.ShapeDtypeStruct((bsz * t, BRANCH), BF16),
        scratch_shapes=[pltpu.VMEM((8 + tt, SSD_XBC), F32),
                        pltpu.VMEM((SSD_GROUPS, SSD_STATE, BRANCH // SSD_GROUPS), F32)],
        compiler_params=_params("parallel", "arbitrary"),
        name="ssd",
    )(z, xbc, small, conv_w, conv_b.reshape(1, -1), pad(dt_bias), pad(a_log),
      jnp.repeat(d_skip, SSD_HEAD_DIM).reshape(1, -1), norm_w.reshape(1, -1))


def _hgrn_body(x_ref, loglb_ref, log1m_ref, onem_ref, nw_ref, o_ref, st_ref, res_ref, oi_ref, *, tt):
    @pl.when(pl.program_id(1) == 0)
    def _():
        st_ref[...] = jnp.zeros_like(st_ref)

    cl, sub = HGRN_CHUNK, HGRN_SUB
    nsub = cl // sub
    tril = _tril(cl)
    t_row = lax.broadcasted_iota(jnp.int32, (sub, cl), 0)
    lane = lax.broadcasted_iota(jnp.int32, (sub, cl), 1)

    def score_phase(sq, r0):
        hq = x_ref[sq, pl.ds(r0, cl), 0 * BRANCH:1 * BRANCH]
        hf = x_ref[sq, pl.ds(r0, cl), 1 * BRANCH:2 * BRANCH]
        hi = x_ref[sq, pl.ds(r0, cl), 2 * BRANCH:3 * BRANCH]
        e = jnp.exp(-jnp.abs(hf))
        d = 1.0 + e
        kk = onem_ref[...] * (jnp.where(hf >= 0.0, e, 1.0) / d)
        la = loglb_ref[...]
        lb2 = log1m_ref[...] + (jnp.minimum(hf, 0.0) - jnp.log(d))
        log_f = jnp.maximum(la, lb2) + jnp.log(1.0 + jnp.exp(-jnp.abs(la - lb2)))
        q = _silu(hq)
        b = _select_dot(tril, log_f * LOG2E)
        b_last = b[cl - 1:cl]
        q_in = (q * jnp.exp2(b)).astype(BF16)
        k_st = (kk * jnp.exp2(b_last - b)).astype(BF16)
        dec = jnp.exp2(b_last)
        v_b = hi.astype(BF16)
        for h in range(HGRN_HEADS):
            sl = slice(h * HGRN_DK, (h + 1) * HGRN_DK)
            bh, qh, kh = b[:, sl], q[:, sl], kk[:, sl]
            st = st_ref[sq, h]
            oi_ref[sq, h] = _dot_nt(q_in[:, sl], st.astype(BF16))
            st_ref[sq, h] = st * dec[:, sl] + _dot_tn(v_b[:, sl], k_st[:, sl])
            for i in range(nsub):
                i0 = i * sub
                bi, qi = bh[i0:i0 + sub], qh[i0:i0 + sub]
                lhs, rhs = [], []
                if i > 0:
                    beta = bh[i0 - 1:i0]
                    lhs.append(qi * jnp.exp2(bi - beta))
                    rhs.append(kh[:i0] * jnp.exp2(beta - bh[:i0]))
                for s in range(sub):
                    lhs.append(qi * jnp.exp2(bi - bi[s:s + 1]))
                rhs.append(kh[i0:i0 + sub])
                if i0 + sub < cl:
                    rhs.append(jnp.zeros((cl - i0 - sub, HGRN_DK), F32))
                res = _dot_nt(jnp.concatenate(lhs, axis=0).astype(BF16),
                              jnp.concatenate(rhs, axis=0).astype(BF16))
                res_ref[sq, h, i, (sub if i == 0 else 0):, :] = res

    def assemble_phase(sq, r0):
        hi = x_ref[sq, pl.ds(r0, cl), 2 * BRANCH:3 * BRANCH]
        hg = x_ref[sq, pl.ds(r0, cl), 3 * BRANCH:4 * BRANCH]
        v_b = hi.astype(BF16)
        outs = []
        for h in range(HGRN_HEADS):
            sl = slice(h * HGRN_DK, (h + 1) * HGRN_DK)
            a_rows = []
            for i in range(nsub):
                i0 = i * sub
                a_i = jnp.where(lane < i0, res_ref[sq, h, i, :sub, :], 0.0) if i > 0 else jnp.zeros((sub, cl), F32)
                for s in range(sub):
                    slab = res_ref[sq, h, i, (s + 1) * sub:(s + 2) * sub, :]
                    a_i = jnp.where((lane == i0 + s) & (t_row >= s), slab, a_i)
                a_rows.append(a_i)
            o_h = oi_ref[sq, h] + _dot(jnp.concatenate(a_rows, axis=0).astype(BF16), v_b[:, sl])
            ms = jnp.mean(o_h * o_h, axis=-1, keepdims=True)
            outs.append(o_h * lax.rsqrt(ms + NORM_EPS))
        o = jnp.concatenate(outs, axis=1) * nw_ref[...]
        o_ref[sq, pl.ds(r0, cl), :] = (o * _silu(hg)).astype(o_ref.dtype)

    def chunk(c, carry):
        r0 = pl.multiple_of(c * cl, cl)
        for sq in range(x_ref.shape[0]):
            score_phase(sq, r0)
        for sq in range(x_ref.shape[0]):
            assemble_phase(sq, r0)
        return carry

    lax.fori_loop(0, tt // cl, chunk, 0)


def _hgrn(proj, lb, norm_w, bsz, t):
    tt = min(256, t)
    nt = t // tt
    nseq = 2 if bsz % 2 == 0 else 1
    blk = lambda b, i: (b, i, 0)
    fixed = lambda b, i: (0, 0)
    lb = lb.reshape(1, BRANCH)
    out = pl.pallas_call(
        functools.partial(_hgrn_body, tt=tt),
        grid=(bsz // nseq, nt),
        in_specs=[pl.BlockSpec((nseq, tt, 4 * BRANCH), blk)] + [pl.BlockSpec((1, BRANCH), fixed)] * 4,
        out_specs=pl.BlockSpec((nseq, tt, BRANCH), blk),
        out_shape=jax.ShapeDtypeStruct((bsz, t, BRANCH), BF16),
        scratch_shapes=[pltpu.VMEM((nseq, HGRN_HEADS, HGRN_DK, HGRN_DK), F32),
                        pltpu.VMEM((nseq, HGRN_HEADS, HGRN_CHUNK // HGRN_SUB, HGRN_SUB * (HGRN_SUB + 1), HGRN_CHUNK), F32),
                        pltpu.VMEM((nseq, HGRN_HEADS, HGRN_CHUNK, HGRN_DK), F32)],
        compiler_params=_params("parallel", "arbitrary"),
        name="hgrn",
    )(proj.reshape(bsz, t, 4 * BRANCH), jnp.log(lb), jnp.log1p(-lb), 1.0 - lb, norm_w.reshape(1, BRANCH))
    return out.reshape(bsz * t, BRANCH)


FOX_BIAS_STRIDE = 8


def _fox_bias_tables():
    sel_q = np.zeros((3 * LANES, BRANCH), np.float32)
    sel_k = np.zeros((3 * LANES, BRANCH), np.float32)
    one_q = np.zeros((1, BRANCH), np.float32)
    one_k = np.zeros((1, BRANCH), np.float32)
    for h in range(FOX_HEADS):
        base = (h // 2) * LANES + FOX_BIAS_STRIDE * (h % 2)
        for j in range(3):
            sel_q[j * LANES + FOX_LANE0 + h, base + j] = 1.0
            sel_k[j * LANES + FOX_LANE0 + h, base + 3 + j] = -1.0
            one_q[0, base + 3 + j] = 1.0
            one_k[0, base + j] = 1.0
    return jnp.asarray(sel_q, BF16), jnp.asarray(sel_k, BF16), jnp.asarray(one_q), jnp.asarray(one_k)


def _fox_prep_body(sm_ref, bias_ref, selq_ref, selk_ref, oneq_ref, onek_ref, qb_ref, kb_ref, pieces_ref, *, t):
    ln = LANES
    tril = _tril(ln)
    carry = jnp.zeros((1, LANES), F32)
    for c in range(t // ln):
        rs = slice(c * ln, (c + 1) * ln)
        lf = _log_sigmoid(sm_ref[rs, :] + bias_ref[...])
        cs = _select_dot(tril, lf) + carry
        carry = cs[ln - 1:ln]
        hi = cs.astype(BF16)
        rest = cs - hi.astype(F32)
        mid = rest.astype(BF16)
        lo = (rest - mid.astype(F32)).astype(BF16)
        pieces_ref[rs, :] = jnp.concatenate([hi, mid, lo], axis=1)
    pieces = pieces_ref[...]
    qb_ref[...] = (_dot(pieces, selq_ref[...]) + oneq_ref[...]).astype(BF16)
    kb_ref[...] = (_dot(pieces, selk_ref[...]) + onek_ref[...]).astype(BF16)


def _fox_prep(small, f_bias, bsz, t):
    bias = jnp.pad(f_bias, (FOX_LANE0, LANES - FOX_LANE0 - FOX_HEADS)).reshape(1, LANES)
    fixed = lambda b: (0, 0)
    out = jax.ShapeDtypeStruct((bsz * t, BRANCH), BF16)
    return pl.pallas_call(
        functools.partial(_fox_prep_body, t=t),
        grid=(bsz,),
        in_specs=[pl.BlockSpec((t, LANES), lambda b: (b, 1)), pl.BlockSpec((1, LANES), fixed),
                  pl.BlockSpec((3 * LANES, BRANCH), fixed), pl.BlockSpec((3 * LANES, BRANCH), fixed),
                  pl.BlockSpec((1, BRANCH), fixed), pl.BlockSpec((1, BRANCH), fixed)],
        out_specs=[pl.BlockSpec((t, BRANCH), lambda b: (b, 0))] * 2,
        out_shape=[out, out],
        scratch_shapes=[pltpu.VMEM((t, 3 * LANES), BF16)],
        compiler_params=_params("parallel"),
        name="fox_prep",
    )(small, bias, *_fox_bias_tables())


FOX_ONES_ROWS = 16


FOX_PAIRS_PER_STEP = 2


def _fox_body(q_ref, k_ref, vt_in_ref, qb_ref, kb_ref, o_ref, *scratch, tq, t):
    qi = pl.program_id(2)
    hd = FOX_HEAD_DIM
    lane = lax.broadcasted_iota(jnp.int32, (tq, LANES), 1)
    sub = lax.broadcasted_iota(jnp.int32, (LANES, tq), 0)
    key = lax.broadcasted_iota(jnp.int32, (tq, 2 * tq), 0)
    qry = lax.broadcasted_iota(jnp.int32, (tq, 2 * tq), 1) % tq
    pair_scratch = [scratch[4 * r:4 * r + 4] for r in range(FOX_PAIRS_PER_STEP)]

    @pl.when(qi == 0)
    def _():
        for r, (kaug_ref, vt_ref, _, _) in enumerate(pair_scratch):
            ls = slice(r * LANES, (r + 1) * LANES)
            kaug_ref[:, :LANES] = k_ref[:, ls]
            kaug_ref[:, LANES:] = kb_ref[:, ls]
            vt_ref[:LANES, :] = vt_in_ref[ls, :]
            vt_ref[LANES:, :] = jnp.ones((FOX_ONES_ROWS, t), BF16)

    q_augs = []
    for r in range(FOX_PAIRS_PER_STEP):
        ls = slice(r * LANES, (r + 1) * LANES)
        q = q_ref[:, ls] * (hd ** -0.5)
        qb = qb_ref[:, ls]
        rows = []
        for h in range(2):
            mine = (lane < hd) if h == 0 else (lane >= hd)
            mine_b = (lane < FOX_BIAS_STRIDE) if h == 0 else (lane >= FOX_BIAS_STRIDE)
            rows.append(jnp.concatenate([jnp.where(mine, q, jnp.zeros_like(q)),
                                         jnp.where(mine_b, qb, jnp.zeros_like(qb))], axis=1))
        q_augs.append(jnp.concatenate(rows, axis=0))

    for c in range(t // tq):
        @pl.when(qi == c)
        def _(c=c):
            nk = (c + 1) * tq
            for r, (kaug_ref, vt_ref, s_ref, p_ref) in enumerate(pair_scratch):
                s_ref[:nk, :] = _dot_nt(kaug_ref[:nk, :], q_augs[r])
            for r, (kaug_ref, vt_ref, s_ref, p_ref) in enumerate(pair_scratch):
                diag = jnp.where(key <= qry, s_ref[c * tq:nk, :], -jnp.inf)
                m = jnp.max(diag, axis=0, keepdims=True)
                if c > 0:
                    m = jnp.maximum(m, jnp.max(s_ref[:c * tq, :], axis=0, keepdims=True))
                    p_ref[:c * tq, :] = jnp.exp(s_ref[:c * tq, :] - m).astype(BF16)
                p_ref[c * tq:nk, :] = jnp.exp(diag - m).astype(BF16)
            for r, (kaug_ref, vt_ref, s_ref, p_ref) in enumerate(pair_scratch):
                acc = _dot(vt_ref[:, :nk], p_ref[:nk, :])
                o0 = acc[:LANES, :tq] / acc[LANES:LANES + 1, :tq]
                o1 = acc[:LANES, tq:] / acc[LANES:LANES + 1, tq:]
                o_ref[:, r * LANES:(r + 1) * LANES] = jnp.where(sub < hd, o0, o1).T.astype(o_ref.dtype)


def _fox(qk, vt, qbias, kbias, bsz, t):
    tq = min(256, t)
    nq = t // tq
    w = FOX_PAIRS_PER_STEP * LANES
    groups = BRANCH // w
    per_pair = [pltpu.VMEM((t, 2 * LANES), BF16), pltpu.VMEM((LANES + FOX_ONES_ROWS, t), BF16),
                pltpu.VMEM((t, 2 * tq), F32), pltpu.VMEM((t, 2 * tq), BF16)]
    return pl.pallas_call(
        functools.partial(_fox_body, tq=tq, t=t),
        grid=(bsz, groups, nq),
        in_specs=[
            pl.BlockSpec((tq, w), lambda b, g, i: (b * nq + i, g)),
            pl.BlockSpec((t, w), lambda b, g, i: (b, groups + g)),
            pl.BlockSpec((w, t), lambda b, g, i: (g, b)),
            pl.BlockSpec((tq, w), lambda b, g, i: (b * nq + i, g)),
            pl.BlockSpec((t, w), lambda b, g, i: (b, g)),
        ],
        out_specs=pl.BlockSpec((tq, w), lambda b, g, i: (b * nq + i, g)),
        out_shape=jax.ShapeDtypeStruct((bsz * t, BRANCH), BF16),
        scratch_shapes=per_pair * FOX_PAIRS_PER_STEP,
        compiler_params=_params("parallel", "parallel", "arbitrary"),
        name="fox",
    )(qk, qk, vt, qbias, kbias)


def _merge_body(y0_ref, y1_ref, y2_ref, w0_ref, w1_ref, w2_ref, g0_ref, g1_ref, g2_ref, o_ref):
    acc = jax.nn.sigmoid(g0_ref[...]) * _dot(y0_ref[...], w0_ref[...])
    acc = acc + jax.nn.sigmoid(g1_ref[...]) * _dot(y1_ref[...], w1_ref[...])
    acc = acc + jax.nn.sigmoid(g2_ref[...]) * _dot(y2_ref[...], w2_ref[...])
    o_ref[...] = acc.astype(o_ref.dtype)


def _merge(ys, ws, gates):
    m = gates.shape[0]
    tm = min(512, m)
    y_spec = pl.BlockSpec((tm, BRANCH), lambda i: (i, 0))
    w_spec = pl.BlockSpec((BRANCH, D_MODEL), lambda i: (0, 0), pipeline_mode=pl.Buffered(1))
    g_specs = [pl.BlockSpec((tm, D_MODEL), functools.partial(lambda i, r: (i, r), r=r)) for r in range(3)]
    return pl.pallas_call(
        _merge_body,
        grid=(m // tm,),
        in_specs=[y_spec] * 3 + [w_spec] * 3 + g_specs,
        out_specs=pl.BlockSpec((tm, D_MODEL), lambda i: (i, 0)),
        out_shape=jax.ShapeDtypeStruct((m, D_MODEL), BF16),
        compiler_params=_params("parallel"),
        name="merge",
    )(*ys, *ws, gates, gates, gates)


def _ffn_body(h_ref, halo_ref, x_ref, nw_ref, wg_ref, wu_ref, cwg_ref, cwu_ref, cbg_ref, cbu_ref, wd_ref,
              *rest, tm, tiles_per_seq, emit_x):
    if emit_x:
        o_ref, n_ref, hh_ref, acc_ref, ug_ref, uu_ref = rest
    else:
        n_ref, hh_ref, acc_ref, ug_ref, uu_ref = rest
    i = pl.program_id(0)
    j = pl.program_id(1)
    pad = 16

    @pl.when(j == 0)
    def _():
        first = (i % tiles_per_seq) == 0
        halo = halo_ref[...]
        hh_ref[:pad, :] = jnp.where(first, jnp.zeros_like(halo), halo)
        hh_ref[pad:, :] = h_ref[...]
        acc_ref[...] = x_ref[...]

    hh = hh_ref[...]
    ug_ref[...] = _dot(hh, wg_ref[...])
    uu_ref[...] = _dot(hh, wu_ref[...])

    def conv(u_ref, cw_ref, cb_ref, cs):
        cw = cw_ref[:, cs]
        out = cb_ref[:, cs] + cw[FFN_CONV - 1:FFN_CONV] * u_ref[pad:, cs]
        for k in range(FFN_CONV - 1):
            back = FFN_CONV - 1 - k
            out = out + cw[k:k + 1] * u_ref[pad - back:pad - back + tm, cs]
        return out

    down = None
    for s0 in range(0, FF_TILE, FF_SUB):
        cs = slice(s0, s0 + FF_SUB)
        act = (_silu(conv(ug_ref, cwg_ref, cbg_ref, cs)) * conv(uu_ref, cwu_ref, cbu_ref, cs)).astype(BF16)
        part = _dot(act, wd_ref[cs, :])
        down = part if down is None else down + part
    acc_ref[...] += down

    @pl.when(j == pl.num_programs(1) - 1)
    def _():
        x_new = acc_ref[...]
        if emit_x:
            o_ref[...] = x_new
        ms = jnp.mean(x_new * x_new, axis=-1, keepdims=True)
        n_ref[...] = (x_new * lax.rsqrt(ms + NORM_EPS) * nw_ref[...]).astype(n_ref.dtype)


def _ffn(h, x, next_norm_w, wg, wu, cwg, cwu, cbg, cbu, wd, t, emit_x, norm_dtype):
    m = x.shape[0]
    tm = min(512, t)
    nf = D_FF_PAD // FF_TILE
    hb = tm // 16
    col = lambda i, j: (0, j)
    row = lambda i, j: (i, 0)
    out_specs = [pl.BlockSpec((tm, D_MODEL), row)]
    out_shape = [jax.ShapeDtypeStruct((m, D_MODEL), norm_dtype)]
    if emit_x:
        out_specs = [pl.BlockSpec((tm, D_MODEL), row)] + out_specs
        out_shape = [jax.ShapeDtypeStruct((m, D_MODEL), F32)] + out_shape
    return pl.pallas_call(
        functools.partial(_ffn_body, tm=tm, tiles_per_seq=t // tm, emit_x=emit_x),
        grid=(m // tm, nf),
        in_specs=[
            pl.BlockSpec((tm, D_MODEL), row),
            pl.BlockSpec((16, D_MODEL), lambda i, j: (jnp.maximum(i * hb - 1, 0), 0)),
            pl.BlockSpec((tm, D_MODEL), row),
            pl.BlockSpec((1, D_MODEL), lambda i, j: (0, 0)),
            pl.BlockSpec((D_MODEL, FF_TILE), col),
            pl.BlockSpec((D_MODEL, FF_TILE), col),
            pl.BlockSpec((FFN_CONV, FF_TILE), col),
            pl.BlockSpec((FFN_CONV, FF_TILE), col),
            pl.BlockSpec((1, FF_TILE), col),
            pl.BlockSpec((1, FF_TILE), col),
            pl.BlockSpec((FF_TILE, D_MODEL), lambda i, j: (j, 0)),
        ],
        out_specs=out_specs,
        out_shape=out_shape,
        scratch_shapes=[pltpu.VMEM((16 + tm, D_MODEL), BF16), pltpu.VMEM((tm, D_MODEL), F32),
                        pltpu.VMEM((16 + tm, FF_TILE), F32), pltpu.VMEM((16 + tm, FF_TILE), F32)],
        compiler_params=_params("parallel", "arbitrary"),
        name="ffn",
    )(h, h, x, next_norm_w.reshape(1, D_MODEL), wg, wu, cwg, cwu, cbg, cbu, wd)


def _split_up_body(w_ref, g_ref, u_ref):
    zeros = jnp.zeros((g_ref.shape[0], D_FF_PAD - D_FF), BF16)
    g_ref[:, :D_FF] = w_ref[:, :D_FF].astype(BF16)
    g_ref[:, D_FF:] = zeros
    u_ref[:, :D_FF] = w_ref[:, D_FF:].astype(BF16)
    u_ref[:, D_FF:] = zeros


def _split_up_weights(w_up_all, l):
    tr = 256
    out = jax.ShapeDtypeStruct((D_MODEL, D_FF_PAD), BF16)
    return pl.pallas_call(
        _split_up_body,
        grid=(D_MODEL // tr,),
        in_specs=[pl.BlockSpec((None, tr, 2 * D_FF), lambda i: (l, i, 0))],
        out_specs=[pl.BlockSpec((tr, D_FF_PAD), lambda i: (i, 0))] * 2,
        out_shape=[out, out],
        compiler_params=_params("parallel"),
        name="split_up_weights",
    )(w_up_all)


def _ffn_weights(w_up_all, l, conv_w, conv_b, w_down):
    padc = lambda a: jnp.pad(a, [(0, 0)] * (a.ndim - 1) + [(0, D_FF_PAD - D_FF)])
    halves = lambda a: (padc(a[..., :D_FF]), padc(a[..., D_FF:]))
    wg, wu = _split_up_weights(w_up_all, l)
    cwg, cwu = halves(conv_w)
    cbg, cbu = halves(conv_b.reshape(1, -1))
    wd = jnp.pad(w_down, ((0, D_FF_PAD - D_FF), (0, 0))).astype(BF16)
    return wg, wu, cwg, cwu, cbg, cbu, wd


def _in_proj_weights(w):
    wt = w.T
    offs = _IN_OFFS
    seg = lambda a, b: wt[offs[a]:offs[b]]
    window = lambda i: wt[offs[i] - offs[i] % LANES:offs[i] - offs[i] % LANES + LANES]
    small = jnp.concatenate([window(2), window(10)], axis=0)
    cast = lambda a: a.astype(BF16)
    return dict(z=cast(seg(0, 1)), xbc=cast(seg(1, 2)), hgrn=cast(seg(3, 7)), fox_qk=cast(seg(7, 9)),
                fox_v=cast(seg(9, 10)), gates=cast(seg(11, 12)), small=cast(small))


def kernel(x, norm_mix_w, w_in, ssd_conv_w, ssd_conv_b, ssd_dt_bias, ssd_a_log, ssd_d, ssd_norm_w, hgrn_lb, hgrn_norm_w, fox_f_bias, w_branch_ssd, w_branch_hgrn, w_branch_fox, w_out, norm_ffn_w, ffn_w_up, ffn_conv_w, ffn_conv_b, ffn_w_down, final_norm_w):
    bsz, t, d = x.shape
    depth = w_in.shape[0]
    lbs = jnp.cumsum(jax.nn.softmax(hgrn_lb.astype(F32), axis=0), axis=0)
    lbs = lbs - lbs[0]
    x = x.reshape(bsz * t, d)
    h = _rmsnorm(x, norm_mix_w[0], BF16)
    for l in range(depth):
        wi = _in_proj_weights(w_in[l])
        proj = lambda key, dtype, tn: _mm(h, wi[key], dtype, tn, "proj_" + key, b_is_nk=True)
        z = proj("z", F32, 1024)
        xbc = proj("xbc", F32, SSD_XBC)
        small = proj("small", F32, 2 * LANES)
        p_hgrn = proj("hgrn", F32, 1024)
        fox_qk = proj("fox_qk", BF16, 1024)
        fox_vt = _mm_t(wi["fox_v"], h, BF16, "proj_fox_vt")
        gates = proj("gates", F32, 1024)
        y_ssd = _ssd(z, xbc, small, ssd_conv_w[l], ssd_conv_b[l], ssd_dt_bias[l], ssd_a_log[l], ssd_d[l],
                     ssd_norm_w[l], bsz, t)
        y_hgrn = _hgrn(p_hgrn, lbs[l], hgrn_norm_w[l], bsz, t)
        y_fox = _fox(fox_qk, fox_vt, *_fox_prep(small, fox_f_bias[l], bsz, t), bsz, t)
        merged = _merge((y_ssd, y_hgrn, y_fox),
                        (w_branch_ssd[l].astype(BF16), w_branch_hgrn[l].astype(BF16), w_branch_fox[l].astype(BF16)),
                        gates)
        x, h = _out_proj(merged, w_out[l].astype(BF16), x, norm_ffn_w[l])
        ffn_w = _ffn_weights(ffn_w_up, l, ffn_conv_w[l], ffn_conv_b[l], ffn_w_down[l])
        if l + 1 < depth:
            x, h = _ffn(h, x, norm_mix_w[l + 1], *ffn_w, t, True, BF16)
        else:
            (out,) = _ffn(h, x, final_norm_w, *ffn_w, t, False, F32)
    return out.reshape(bsz, t, d)
```

```python
import functools

import jax
import jax.numpy as jnp
import numpy as np
from jax import lax
from jax.experimental import pallas as pl
from jax.experimental.pallas import tpu as pltpu

F32 = jnp.float32
BF16 = jnp.bfloat16
LOG2E = 1.4426950408889634

D_MODEL = 2048
BRANCH = 1024
SSD_HEADS = 16
SSD_HEAD_DIM = 64
SSD_GROUPS = 2
SSD_STATE = 128
SSD_CONV = 4
SSD_CHUNK = 128
SSD_XBC = BRANCH + 2 * SSD_GROUPS * SSD_STATE
HGRN_HEADS = 8
HGRN_DK = 128
HGRN_CHUNK = 64
HGRN_SUB = 8
FOX_HEADS = 16
FOX_HEAD_DIM = 64
D_FF = 5504
FFN_CONV = 3
NORM_EPS = 1e-6
IN_SIZES = (BRANCH, SSD_XBC, SSD_HEADS, BRANCH, BRANCH, BRANCH, BRANCH, BRANCH, BRANCH, BRANCH,
            FOX_HEADS, 3 * D_MODEL)

LANES = 128
FF_TILE = 512
FF_SUB = 256
D_FF_PAD = 5632
VMEM_LIMIT = 56 * 1024 * 1024

_IN_OFFS = [sum(IN_SIZES[:i]) for i in range(len(IN_SIZES) + 1)]
DT_LANE0 = _IN_OFFS[2] % LANES
FOX_LANE0 = _IN_OFFS[10] % LANES
assert DT_LANE0 == 0 and FOX_LANE0 + FOX_HEADS <= LANES


def _params(*sem):
    return pltpu.CompilerParams(dimension_semantics=sem, vmem_limit_bytes=VMEM_LIMIT)


def _silu(x):
    return x * jax.nn.sigmoid(x)


def _log_sigmoid(x):
    return jnp.minimum(x, 0.0) - jnp.log(1.0 + jnp.exp(-jnp.abs(x)))


def _dot(a, b, **kw):
    return jnp.dot(a, b, preferred_element_type=F32, **kw)


def _dot_nt(a, b):
    return lax.dot_general(a, b, (((1,), (1,)), ((), ())), preferred_element_type=F32)


def _dot_tn(a, b):
    return lax.dot_general(a, b, (((0,), (0,)), ((), ())), preferred_element_type=F32)


def _tril(n):
    r = lax.broadcasted_iota(jnp.int32, (n, n), 0)
    c = lax.broadcasted_iota(jnp.int32, (n, n), 1)
    return (r >= c).astype(BF16)


def _split3(x):
    hi = x.astype(BF16)
    rest = x - hi.astype(F32)
    mid = rest.astype(BF16)
    lo = (rest - mid.astype(F32)).astype(BF16)
    return hi, mid, lo


def _select_dot(sel, x):
    return _dot(jnp.concatenate([sel, sel, sel], axis=1), jnp.concatenate(_split3(x), axis=0))


def _dot_select(x, sel):
    return _dot(jnp.concatenate(_split3(x), axis=1), jnp.concatenate([sel, sel, sel], axis=0))


def _rmsnorm_body(x_ref, w_ref, o_ref):
    x = x_ref[...]
    ms = jnp.mean(x * x, axis=-1, keepdims=True)
    o_ref[...] = (x * lax.rsqrt(ms + NORM_EPS) * w_ref[...]).astype(o_ref.dtype)


def _rmsnorm(x, w, out_dtype):
    m, d = x.shape
    tm = min(512, m)
    return pl.pallas_call(
        _rmsnorm_body,
        grid=(m // tm,),
        in_specs=[pl.BlockSpec((tm, d), lambda i: (i, 0)), pl.BlockSpec((1, d), lambda i: (0, 0))],
        out_specs=pl.BlockSpec((tm, d), lambda i: (i, 0)),
        out_shape=jax.ShapeDtypeStruct((m, d), out_dtype),
        compiler_params=_params("parallel"),
        name="rmsnorm",
    )(x, w.reshape(1, d))


def _mm_body(a_ref, b_ref, o_ref):
    o_ref[...] = _dot(a_ref[...], b_ref[...]).astype(o_ref.dtype)


def _mm_nt_body(a_ref, b_ref, o_ref):
    o_ref[...] = _dot_nt(a_ref[...], b_ref[...]).astype(o_ref.dtype)


def _out_proj_body(a_ref, b_ref, r_ref, nw_ref, o_ref, h_ref):
    x = r_ref[...] + _dot(a_ref[...], b_ref[...])
    o_ref[...] = x
    ms = jnp.mean(x * x, axis=-1, keepdims=True)
    h_ref[...] = (x * lax.rsqrt(ms + NORM_EPS) * nw_ref[...]).astype(h_ref.dtype)


def _out_proj(a, w, x, norm_w):
    m, k = a.shape
    d = w.shape[1]
    tm = min(512, m)
    row = lambda i: (i, 0)
    fixed = lambda i: (0, 0)
    return pl.pallas_call(
        _out_proj_body,
        grid=(m // tm,),
        in_specs=[pl.BlockSpec((tm, k), row), pl.BlockSpec((k, d), fixed, pipeline_mode=pl.Buffered(1)),
                  pl.BlockSpec((tm, d), row), pl.BlockSpec((1, d), fixed)],
        out_specs=[pl.BlockSpec((tm, d), row), pl.BlockSpec((tm, d), row)],
        out_shape=[jax.ShapeDtypeStruct((m, d), F32), jax.ShapeDtypeStruct((m, d), BF16)],
        compiler_params=_params("parallel"),
        name="out_proj",
    )(a, w, x, norm_w.reshape(1, d))


def _mm_t_body(w_ref, a_ref, o_ref):
    o_ref[...] = _dot_nt(w_ref[...], a_ref[...]).astype(o_ref.dtype)


def _mm_t(w, a, out_dtype, name, tm=1024):
    n, k = w.shape
    m = a.shape[0]
    tm = min(tm, m)
    return pl.pallas_call(
        _mm_t_body,
        grid=(m // tm,),
        in_specs=[pl.BlockSpec((n, k), lambda i: (0, 0), pipeline_mode=pl.Buffered(1)),
                  pl.BlockSpec((tm, k), lambda i: (i, 0))],
        out_specs=pl.BlockSpec((n, tm), lambda i: (0, i)),
        out_shape=jax.ShapeDtypeStruct((n, m), out_dtype),
        compiler_params=_params("parallel"),
        name=name,
    )(w, a)


def _mm(a, b, out_dtype, tn, name, b_is_nk=False, tm=1024):
    m, k = a.shape
    n = b.shape[0] if b_is_nk else b.shape[1]
    tm = min(tm, m)
    mode = dict(pipeline_mode=pl.Buffered(1)) if n == tn else {}
    b_spec = (pl.BlockSpec((tn, k), lambda i, j: (j, 0), **mode) if b_is_nk
              else pl.BlockSpec((k, tn), lambda i, j: (0, j), **mode))
    in_specs = [pl.BlockSpec((tm, k), lambda i, j: (i, 0)), b_spec]
    args = [a, b]
    body = _mm_nt_body if b_is_nk else _mm_body
    return pl.pallas_call(
        body,
        grid=(m // tm, n // tn),
        in_specs=in_specs,
        out_specs=pl.BlockSpec((tm, tn), lambda i, j: (i, j)),
        out_shape=jax.ShapeDtypeStruct((m, n), out_dtype),
        compiler_params=_params("parallel", "parallel"),
        name=name,
    )(*args)


def _ssd_body(z_ref, xbc_ref, sm_ref, cw_ref, cb_ref, dtb_ref, alog_ref, d_ref, nw_ref, o_ref,
              ext_ref, s_ref, *, tt):
    @pl.when(pl.program_id(1) == 0)
    def _():
        ext_ref[:8, :] = jnp.zeros((8, SSD_XBC), F32)
        s_ref[...] = jnp.zeros_like(s_ref)

    ext_ref[8:, :] = xbc_ref[...]
    cw = cw_ref[...]
    conv = cb_ref[...] + cw[SSD_CONV - 1:SSD_CONV] * xbc_ref[...]
    for k in range(SSD_CONV - 1):
        back = SSD_CONV - 1 - k
        conv = conv + cw[k:k + 1] * ext_ref[8 - back:8 - back + tt, :]
    ext_ref[:8, :] = xbc_ref[tt - 8:, :]
    act = _silu(conv)
    xs_all = act[:, :BRANCH]
    dt_all = jax.nn.softplus(sm_ref[...] + dtb_ref[...])
    adt_all = dt_all * (-jnp.exp(alog_ref[...]))

    ln = SSD_CHUNK
    tril = _tril(ln)
    row = lax.broadcasted_iota(jnp.int32, (ln, ln), 0)
    col = lax.broadcasted_iota(jnp.int32, (ln, ln), 1)
    causal = row >= col
    lane = lax.broadcasted_iota(jnp.int32, (ln, LANES), 1)
    eh = lax.broadcasted_iota(jnp.int32, (LANES, BRANCH), 0)
    ec = lax.broadcasted_iota(jnp.int32, (LANES, BRANCH), 1)
    expand = (ec // SSD_HEAD_DIM == eh).astype(BF16)
    gw = BRANCH // SSD_GROUPS

    for c in range(tt // ln):
        rs = slice(c * ln, (c + 1) * ln)
        xs = xs_all[rs]
        cs = _select_dot(tril, adt_all[rs])
        cs_t = cs.T
        cs_x = _dot_select(cs, expand)
        dt_x = _dot_select(dt_all[rs], expand)
        xdt = xs * dt_x
        last_x = cs_x[ln - 1:ln]
        xdt_st = (xdt * jnp.exp(last_x - cs_x)).astype(BF16)
        out_decay = jnp.exp(cs_x)
        chunk_decay = jnp.exp(last_x)
        xdt_b = xdt.astype(BF16)
        ys = []
        for g in range(SSD_GROUPS):
            b_g = act[rs, BRANCH + g * SSD_STATE:BRANCH + (g + 1) * SSD_STATE].astype(BF16)
            c_g = act[rs, BRANCH + (SSD_GROUPS + g) * SSD_STATE:
                      BRANCH + (SSD_GROUPS + g + 1) * SSD_STATE].astype(BF16)
            gs = slice(g * gw, (g + 1) * gw)
            cb = _dot_nt(c_g, b_g)
            state = s_ref[g]
            y_off = _dot(c_g, state.astype(BF16)) * out_decay[:, gs]
            s_ref[g] = state * chunk_decay[:, gs] + _dot_tn(b_g, xdt_st[:, gs])
            heads_per_group = SSD_HEADS // SSD_GROUPS
            pieces = []
            for p in range(heads_per_group // 2):
                h0 = g * heads_per_group + 2 * p
                xp = xdt_b[:, h0 * SSD_HEAD_DIM:(h0 + 2) * SSD_HEAD_DIM]
                res = []
                for h in (h0, h0 + 1):
                    seg = jnp.where(causal, cs[:, h:h + 1] - cs_t[h:h + 1, :], -jnp.inf)
                    res.append(_dot((cb * jnp.exp(seg)).astype(BF16), xp))
                pieces.append(jnp.where(lane < SSD_HEAD_DIM, res[0], res[1]))
            ys.append(jnp.concatenate(pieces, axis=1) + y_off)
        y = jnp.concatenate(ys, axis=1) + xs * d_ref[...]
        y = y * _silu(z_ref[rs, :])
        normed = []
        for g in range(SSD_GROUPS):
            yg = y[:, g * gw:(g + 1) * gw]
            ms = jnp.mean(yg * yg, axis=-1, keepdims=True)
            normed.append(yg * lax.rsqrt(ms + NORM_EPS))
        o_ref[rs, :] = (jnp.concatenate(normed, axis=1) * nw_ref[...]).astype(o_ref.dtype)


def _ssd(z, xbc, small, conv_w, conv_b, dt_bias, a_log, d_skip, norm_w, bsz, t):
    tt = min(512, t)
    nt = t // tt
    row = lambda b, i: (b * nt + i, 0)
    fixed = lambda b, i: (0, 0)
    pad = lambda v: jnp.pad(v, (0, LANES - SSD_HEADS)).reshape(1, LANES)
    return pl.pallas_call(
        functools.partial(_ssd_body, tt=tt),
        grid=(bsz, nt),
        in_specs=[
            pl.BlockSpec((tt, BRANCH), row),
            pl.BlockSpec((tt, SSD_XBC), row),
            pl.BlockSpec((tt, LANES), row),
            pl.BlockSpec((SSD_CONV, SSD_XBC), fixed),
            pl.BlockSpec((1, SSD_XBC), fixed),
            pl.BlockSpec((1, LANES), fixed),
            pl.BlockSpec((1, LANES), fixed),
            pl.BlockSpec((1, BRANCH), fixed),
            pl.BlockSpec((1, BRANCH), fixed),
        ],
        out_specs=pl.BlockSpec((tt, BRANCH), row),
        out_shape=jax.ShapeDtypeStruct((bsz * t, BRANCH), BF16),
        scratch_shapes=[pltpu.VMEM((8 + tt, SSD_XBC), F32),
                        pltpu.VMEM((SSD_GROUPS, SSD_STATE, BRANCH // SSD_GROUPS), F32)],
        compiler_params=_params("parallel", "arbitrary"),
        name="ssd",
    )(z, xbc, small, conv_w, conv_b.reshape(1, -1), pad(dt_bias), pad(a_log),
      jnp.repeat(d_skip, SSD_HEAD_DIM).reshape(1, -1), norm_w.reshape(1, -1))


def _hgrn_body(x_ref, loglb_ref, log1m_ref, onem_ref, nw_ref, o_ref, st_ref, res_ref, oi_ref, *, tt):
    @pl.when(pl.program_id(1) == 0)
    def _():
        st_ref[...] = jnp.zeros_like(st_ref)

    cl, sub = HGRN_CHUNK, HGRN_SUB
    nsub = cl // sub
    tril = _tril(cl)
    t_row = lax.broadcasted_iota(jnp.int32, (sub, cl), 0)
    lane = lax.broadcasted_iota(jnp.int32, (sub, cl), 1)

    def score_phase(sq, r0):
        hq = x_ref[sq, pl.ds(r0, cl), 0 * BRANCH:1 * BRANCH]
        hf = x_ref[sq, pl.ds(r0, cl), 1 * BRANCH:2 * BRANCH]
        hi = x_ref[sq, pl.ds(r0, cl), 2 * BRANCH:3 * BRANCH]
        e = jnp.exp(-jnp.abs(hf))
        d = 1.0 + e
        kk = onem_ref[...] * (jnp.where(hf >= 0.0, e, 1.0) / d)
        la = loglb_ref[...]
        lb2 = log1m_ref[...] + (jnp.minimum(hf, 0.0) - jnp.log(d))
        log_f = jnp.maximum(la, lb2) + jnp.log(1.0 + jnp.exp(-jnp.abs(la - lb2)))
        q = _silu(hq)
        b = _select_dot(tril, log_f * LOG2E)
        b_last = b[cl - 1:cl]
        q_in = (q * jnp.exp2(b)).astype(BF16)
        k_st = (kk * jnp.exp2(b_last - b)).astype(BF16)
        dec = jnp.exp2(b_last)
        v_b = hi.astype(BF16)
        for h in range(HGRN_HEADS):
            sl = slice(h * HGRN_DK, (h + 1) * HGRN_DK)
            bh, qh, kh = b[:, sl], q[:, sl], kk[:, sl]
            st = st_ref[sq, h]
            oi_ref[sq, h] = _dot_nt(q_in[:, sl], st.astype(BF16))
            st_ref[sq, h] = st * dec[:, sl] + _dot_tn(v_b[:, sl], k_st[:, sl])
            for i in range(nsub):
                i0 = i * sub
                bi, qi = bh[i0:i0 + sub], qh[i0:i0 + sub]
                lhs, rhs = [], []
                if i > 0:
                    beta = bh[i0 - 1:i0]
                    lhs.append(qi * jnp.exp2(bi - beta))
                    rhs.append(kh[:i0] * jnp.exp2(beta - bh[:i0]))
                for s in range(sub):
                    lhs.append(qi * jnp.exp2(bi - bi[s:s + 1]))
                rhs.append(kh[i0:i0 + sub])
                if i0 + sub < cl:
                    rhs.append(jnp.zeros((cl - i0 - sub, HGRN_DK), F32))
                res = _dot_nt(jnp.concatenate(lhs, axis=0).astype(BF16),
                              jnp.concatenate(rhs, axis=0).astype(BF16))
                res_ref[sq, h, i, (sub if i == 0 else 0):, :] = res

    def assemble_phase(sq, r0):
        hi = x_ref[sq, pl.ds(r0, cl), 2 * BRANCH:3 * BRANCH]
        hg = x_ref[sq, pl.ds(r0, cl), 3 * BRANCH:4 * BRANCH]
        v_b = hi.astype(BF16)
        outs = []
        for h in range(HGRN_HEADS):
            sl = slice(h * HGRN_DK, (h + 1) * HGRN_DK)
            a_rows = []
            for i in range(nsub):
                i0 = i * sub
                a_i = jnp.where(lane < i0, res_ref[sq, h, i, :sub, :], 0.0) if i > 0 else jnp.zeros((sub, cl), F32)
                for s in range(sub):
                    slab = res_ref[sq, h, i, (s + 1) * sub:(s + 2) * sub, :]
                    a_i = jnp.where((lane == i0 + s) & (t_row >= s), slab, a_i)
                a_rows.append(a_i)
            o_h = oi_ref[sq, h] + _dot(jnp.concatenate(a_rows, axis=0).astype(BF16), v_b[:, sl])
            ms = jnp.mean(o_h * o_h, axis=-1, keepdims=True)
            outs.append(o_h * lax.rsqrt(ms + NORM_EPS))
        o = jnp.concatenate(outs, axis=1) * nw_ref[...]
        o_ref[sq, pl.ds(r0, cl), :] = (o * _silu(hg)).astype(o_ref.dtype)

    def chunk(c, carry):
        r0 = pl.multiple_of(c * cl, cl)
        for sq in range(x_ref.shape[0]):
            score_phase(sq, r0)
        for sq in range(x_ref.shape[0]):
            assemble_phase(sq, r0)
        return carry

    lax.fori_loop(0, tt // cl, chunk, 0)


def _hgrn(proj, lb, norm_w, bsz, t):
    tt = min(256, t)
    nt = t // tt
    nseq = 2 if bsz % 2 == 0 else 1
    blk = lambda b, i: (b, i, 0)
    fixed = lambda b, i: (0, 0)
    lb = lb.reshape(1, BRANCH)
    out = pl.pallas_call(
        functools.partial(_hgrn_body, tt=tt),
        grid=(bsz // nseq, nt),
        in_specs=[pl.BlockSpec((nseq, tt, 4 * BRANCH), blk)] + [pl.BlockSpec((1, BRANCH), fixed)] * 4,
        out_specs=pl.BlockSpec((nseq, tt, BRANCH), blk),
        out_shape=jax.ShapeDtypeStruct((bsz, t, BRANCH), BF16),
        scratch_shapes=[pltpu.VMEM((nseq, HGRN_HEADS, HGRN_DK, HGRN_DK), F32),
                        pltpu.VMEM((nseq, HGRN_HEADS, HGRN_CHUNK // HGRN_SUB, HGRN_SUB * (HGRN_SUB + 1), HGRN_CHUNK), F32),
                        pltpu.VMEM((nseq, HGRN_HEADS, HGRN_CHUNK, HGRN_DK), F32)],
        compiler_params=_params("parallel", "arbitrary"),
        name="hgrn",
    )(proj.reshape(bsz, t, 4 * BRANCH), jnp.log(lb), jnp.log1p(-lb), 1.0 - lb, norm_w.reshape(1, BRANCH))
    return out.reshape(bsz * t, BRANCH)


FOX_BIAS_STRIDE = 8


def _fox_bias_tables():
    sel_q = np.zeros((3 * LANES, BRANCH), np.float32)
    sel_k = np.zeros((3 * LANES, BRANCH), np.float32)
    one_q = np.zeros((1, BRANCH), np.float32)
    one_k = np.zeros((1, BRANCH), np.float32)
    for h in range(FOX_HEADS):
        base = (h // 2) * LANES + FOX_BIAS_STRIDE * (h % 2)
        for j in range(3):
            sel_q[j * LANES + FOX_LANE0 + h, base + j] = 1.0
            sel_k[j * LANES + FOX_LANE0 + h, base + 3 + j] = -1.0
            one_q[0, base + 3 + j] = 1.0
            one_k[0, base + j] = 1.0
    return jnp.asarray(sel_q, BF16), jnp.asarray(sel_k, BF16), jnp.asarray(one_q), jnp.asarray(one_k)


def _fox_prep_body(sm_ref, bias_ref, selq_ref, selk_ref, oneq_ref, onek_ref, qb_ref, kb_ref, pieces_ref, *, t):
    ln = LANES
    tril = _tril(ln)
    carry = jnp.zeros((1, LANES), F32)
    for c in range(t // ln):
        rs = slice(c * ln, (c + 1) * ln)
        lf = _log_sigmoid(sm_ref[rs, :] + bias_ref[...])
        cs = _select_dot(tril, lf) + carry
        carry = cs[ln - 1:ln]
        hi = cs.astype(BF16)
        rest = cs - hi.astype(F32)
        mid = rest.astype(BF16)
        lo = (rest - mid.astype(F32)).astype(BF16)
        pieces_ref[rs, :] = jnp.concatenate([hi, mid, lo], axis=1)
    pieces = pieces_ref[...]
    qb_ref[...] = (_dot(pieces, selq_ref[...]) + oneq_ref[...]).astype(BF16)
    kb_ref[...] = (_dot(pieces, selk_ref[...]) + onek_ref[...]).astype(BF16)


def _fox_prep(small, f_bias, bsz, t):
    bias = jnp.pad(f_bias, (FOX_LANE0, LANES - FOX_LANE0 - FOX_HEADS)).reshape(1, LANES)
    fixed = lambda b: (0, 0)
    out = jax.ShapeDtypeStruct((bsz * t, BRANCH), BF16)
    return pl.pallas_call(
        functools.partial(_fox_prep_body, t=t),
        grid=(bsz,),
        in_specs=[pl.BlockSpec((t, LANES), lambda b: (b, 1)), pl.BlockSpec((1, LANES), fixed),
                  pl.BlockSpec((3 * LANES, BRANCH), fixed), pl.BlockSpec((3 * LANES, BRANCH), fixed),
                  pl.BlockSpec((1, BRANCH), fixed), pl.BlockSpec((1, BRANCH), fixed)],
        out_specs=[pl.BlockSpec((t, BRANCH), lambda b: (b, 0))] * 2,
        out_shape=[out, out],
        scratch_shapes=[pltpu.VMEM((t, 3 * LANES), BF16)],
        compiler_params=_params("parallel"),
        name="fox_prep",
    )(small, bias, *_fox_bias_tables())


FOX_ONES_ROWS = 16


FOX_PAIRS_PER_STEP = 4


def _fox_body(q_ref, k_ref, vt_in_ref, qb_ref, kb_ref, o_ref, *scratch, tq, t):
    qi = pl.program_id(2)
    hd = FOX_HEAD_DIM
    lane = lax.broadcasted_iota(jnp.int32, (tq, LANES), 1)
    sub = lax.broadcasted_iota(jnp.int32, (LANES, tq), 0)
    key = lax.broadcasted_iota(jnp.int32, (tq, 2 * tq), 0)
    qry = lax.broadcasted_iota(jnp.int32, (tq, 2 * tq), 1) % tq
    pair_scratch = [scratch[4 * r:4 * r + 4] for r in range(FOX_PAIRS_PER_STEP)]

    @pl.when(qi == 0)
    def _():
        for r, (kaug_ref, vt_ref, _, _) in enumerate(pair_scratch):
            ls = slice(r * LANES, (r + 1) * LANES)
            kaug_ref[:, :LANES] = k_ref[:, ls]
            kaug_ref[:, LANES:] = kb_ref[:, ls]
            vt_ref[:LANES, :] = vt_in_ref[ls, :]
            vt_ref[LANES:, :] = jnp.ones((FOX_ONES_ROWS, t), BF16)

    q_augs = []
    for r in range(FOX_PAIRS_PER_STEP):
        ls = slice(r * LANES, (r + 1) * LANES)
        q = q_ref[:, ls] * (hd ** -0.5)
        qb = qb_ref[:, ls]
        rows = []
        for h in range(2):
            mine = (lane < hd) if h == 0 else (lane >= hd)
            mine_b = (lane < FOX_BIAS_STRIDE) if h == 0 else (lane >= FOX_BIAS_STRIDE)
            rows.append(jnp.concatenate([jnp.where(mine, q, jnp.zeros_like(q)),
                                         jnp.where(mine_b, qb, jnp.zeros_like(qb))], axis=1))
        q_augs.append(jnp.concatenate(rows, axis=0))

    for c in range(t // tq):
        @pl.when(qi == c)
        def _(c=c):
            nk = (c + 1) * tq
            for r, (kaug_ref, vt_ref, s_ref, p_ref) in enumerate(pair_scratch):
                s_ref[:nk, :] = _dot_nt(kaug_ref[:nk, :], q_augs[r])
            for r, (kaug_ref, vt_ref, s_ref, p_ref) in enumerate(pair_scratch):
                diag = jnp.where(key <= qry, s_ref[c * tq:nk, :], -jnp.inf)
                m = jnp.max(diag, axis=0, keepdims=True)
                if c > 0:
                    m = jnp.maximum(m, jnp.max(s_ref[:c * tq, :], axis=0, keepdims=True))
                    p_ref[:c * tq, :] = jnp.exp(s_ref[:c * tq, :] - m).astype(BF16)
                p_ref[c * tq:nk, :] = jnp.exp(diag - m).astype(BF16)
            for r, (kaug_ref, vt_ref, s_ref, p_ref) in enumerate(pair_scratch):
                acc = _dot(vt_ref[:, :nk], p_ref[:nk, :])
                o0 = acc[:LANES, :tq] / acc[LANES:LANES + 1, :tq]
                o1 = acc[:LANES, tq:] / acc[LANES:LANES + 1, tq:]
                o_ref[:, r * LANES:(r + 1) * LANES] = jnp.where(sub < hd, o0, o1).T.astype(o_ref.dtype)


def _fox(qk, vt, qbias, kbias, bsz, t):
    tq = min(256, t)
    nq = t // tq
    w = FOX_PAIRS_PER_STEP * LANES
    groups = BRANCH // w
    per_pair = [pltpu.VMEM((t, 2 * LANES), BF16), pltpu.VMEM((LANES + FOX_ONES_ROWS, t), BF16),
                pltpu.VMEM((t, 2 * tq), F32), pltpu.VMEM((t, 2 * tq), BF16)]
    return pl.pallas_call(
        functools.partial(_fox_body, tq=tq, t=t),
        grid=(bsz, groups, nq),
        in_specs=[
            pl.BlockSpec((tq, w), lambda b, g, i: (b * nq + i, g)),
            pl.BlockSpec((t, w), lambda b, g, i: (b, groups + g)),
            pl.BlockSpec((w, t), lambda b, g, i: (g, b)),
            pl.BlockSpec((tq, w), lambda b, g, i: (b * nq + i, g)),
            pl.BlockSpec((t, w), lambda b, g, i: (b, g)),
        ],
        out_specs=pl.BlockSpec((tq, w), lambda b, g, i: (b * nq + i, g)),
        out_shape=jax.ShapeDtypeStruct((bsz * t, BRANCH), BF16),
        scratch_shapes=per_pair * FOX_PAIRS_PER_STEP,
        compiler_params=_params("parallel", "parallel", "arbitrary"),
        name="fox",
    )(qk, qk, vt, qbias, kbias)


def _merge_body(y0_ref, y1_ref, y2_ref, w0_ref, w1_ref, w2_ref, g0_ref, g1_ref, g2_ref, o_ref):
    acc = jax.nn.sigmoid(g0_ref[...]) * _dot(y0_ref[...], w0_ref[...])
    acc = acc + jax.nn.sigmoid(g1_ref[...]) * _dot(y1_ref[...], w1_ref[...])
    acc = acc + jax.nn.sigmoid(g2_ref[...]) * _dot(y2_ref[...], w2_ref[...])
    o_ref[...] = acc.astype(o_ref.dtype)


def _merge(ys, ws, gates):
    m = gates.shape[0]
    tm = min(512, m)
    y_spec = pl.BlockSpec((tm, BRANCH), lambda i: (i, 0))
    w_spec = pl.BlockSpec((BRANCH, D_MODEL), lambda i: (0, 0), pipeline_mode=pl.Buffered(1))
    g_specs = [pl.BlockSpec((tm, D_MODEL), functools.partial(lambda i, r: (i, r), r=r)) for r in range(3)]
    return pl.pallas_call(
        _merge_body,
        grid=(m // tm,),
        in_specs=[y_spec] * 3 + [w_spec] * 3 + g_specs,
        out_specs=pl.BlockSpec((tm, D_MODEL), lambda i: (i, 0)),
        out_shape=jax.ShapeDtypeStruct((m, D_MODEL), BF16),
        compiler_params=_params("parallel"),
        name="merge",
    )(*ys, *ws, gates, gates, gates)


def _ffn_body(h_ref, x_ref, nw_ref, wg_ref, wu_ref, cwg_ref, cwu_ref, cbg_ref, cbu_ref, wd_ref,
              *rest, tm, tiles_per_seq, emit_x):
    if emit_x:
        o_ref, n_ref, acc_ref, ug_ref, uu_ref, carry_ref = rest
    else:
        n_ref, acc_ref, ug_ref, uu_ref, carry_ref = rest
    i = pl.program_id(0)
    j = pl.program_id(1)
    pad = 8
    first = (i % tiles_per_seq) == 0

    @pl.when(j == 0)
    def _():
        acc_ref[...] = x_ref[...]

    @pl.when(first)
    def _():
        ug_ref[:pad, :] = jnp.zeros((pad, FF_TILE), F32)
        uu_ref[:pad, :] = jnp.zeros((pad, FF_TILE), F32)

    @pl.when(jnp.logical_not(first))
    def _():
        ug_ref[:pad, :] = carry_ref[j, 0]
        uu_ref[:pad, :] = carry_ref[j, 1]

    hh = h_ref[...]
    ug_ref[pad:, :] = _dot(hh, wg_ref[...])
    uu_ref[pad:, :] = _dot(hh, wu_ref[...])
    carry_ref[j, 0] = ug_ref[tm:, :]
    carry_ref[j, 1] = uu_ref[tm:, :]

    def conv(u_ref, cw_ref, cb_ref, cs):
        cw = cw_ref[:, cs]
        out = cb_ref[:, cs] + cw[FFN_CONV - 1:FFN_CONV] * u_ref[pad:, cs]
        for k in range(FFN_CONV - 1):
            back = FFN_CONV - 1 - k
            out = out + cw[k:k + 1] * u_ref[pad - back:pad - back + tm, cs]
        return out

    down = None
    for s0 in range(0, FF_TILE, FF_SUB):
        cs = slice(s0, s0 + FF_SUB)
        act = (_silu(conv(ug_ref, cwg_ref, cbg_ref, cs)) * conv(uu_ref, cwu_ref, cbu_ref, cs)).astype(BF16)
        part = _dot(act, wd_ref[cs, :])
        down = part if down is None else down + part
    acc_ref[...] += down

    @pl.when(j == pl.num_programs(1) - 1)
    def _():
        x_new = acc_ref[...]
        if emit_x:
            o_ref[...] = x_new
        ms = jnp.mean(x_new * x_new, axis=-1, keepdims=True)
        n_ref[...] = (x_new * lax.rsqrt(ms + NORM_EPS) * nw_ref[...]).astype(n_ref.dtype)


def _ffn(h, x, next_norm_w, wg, wu, cwg, cwu, cbg, cbu, wd, t, emit_x, norm_dtype):
    m = x.shape[0]
    tm = min(512, t)
    nf = D_FF_PAD // FF_TILE
    col = lambda i, j: (0, j)
    row = lambda i, j: (i, 0)
    out_specs = [pl.BlockSpec((tm, D_MODEL), row)]
    out_shape = [jax.ShapeDtypeStruct((m, D_MODEL), norm_dtype)]
    if emit_x:
        out_specs = [pl.BlockSpec((tm, D_MODEL), row)] + out_specs
        out_shape = [jax.ShapeDtypeStruct((m, D_MODEL), F32)] + out_shape
    return pl.pallas_call(
        functools.partial(_ffn_body, tm=tm, tiles_per_seq=t // tm, emit_x=emit_x),
        grid=(m // tm, nf),
        in_specs=[
            pl.BlockSpec((tm, D_MODEL), row),
            pl.BlockSpec((tm, D_MODEL), row),
            pl.BlockSpec((1, D_MODEL), lambda i, j: (0, 0)),
            pl.BlockSpec((D_MODEL, FF_TILE), col),
            pl.BlockSpec((D_MODEL, FF_TILE), col),
            pl.BlockSpec((FFN_CONV, FF_TILE), col),
            pl.BlockSpec((FFN_CONV, FF_TILE), col),
            pl.BlockSpec((1, FF_TILE), col),
            pl.BlockSpec((1, FF_TILE), col),
            pl.BlockSpec((FF_TILE, D_MODEL), lambda i, j: (j, 0)),
        ],
        out_specs=out_specs,
        out_shape=out_shape,
        scratch_shapes=[pltpu.VMEM((tm, D_MODEL), F32),
                        pltpu.VMEM((8 + tm, FF_TILE), F32), pltpu.VMEM((8 + tm, FF_TILE), F32),
                        pltpu.VMEM((nf, 2, 8, FF_TILE), F32)],
        compiler_params=_params("arbitrary", "arbitrary"),
        name="ffn",
    )(h, x, next_norm_w.reshape(1, D_MODEL), wg, wu, cwg, cwu, cbg, cbu, wd)


def _split_up_body(w_ref, g_ref, u_ref):
    zeros = jnp.zeros((g_ref.shape[0], D_FF_PAD - D_FF), BF16)
    g_ref[:, :D_FF] = w_ref[:, :D_FF].astype(BF16)
    g_ref[:, D_FF:] = zeros
    u_ref[:, :D_FF] = w_ref[:, D_FF:].astype(BF16)
    u_ref[:, D_FF:] = zeros


def _split_up_weights(w_up_all, l):
    tr = 256
    out = jax.ShapeDtypeStruct((D_MODEL, D_FF_PAD), BF16)
    return pl.pallas_call(
        _split_up_body,
        grid=(D_MODEL // tr,),
        in_specs=[pl.BlockSpec((None, tr, 2 * D_FF), lambda i: (l, i, 0))],
        out_specs=[pl.BlockSpec((tr, D_FF_PAD), lambda i: (i, 0))] * 2,
        out_shape=[out, out],
        compiler_params=_params("parallel"),
        name="split_up_weights",
    )(w_up_all)


def _ffn_weights(w_up_all, l, conv_w, conv_b, w_down):
    padc = lambda a: jnp.pad(a, [(0, 0)] * (a.ndim - 1) + [(0, D_FF_PAD - D_FF)])
    halves = lambda a: (padc(a[..., :D_FF]), padc(a[..., D_FF:]))
    wg, wu = _split_up_weights(w_up_all, l)
    cwg, cwu = halves(conv_w)
    cbg, cbu = halves(conv_b.reshape(1, -1))
    wd = jnp.pad(w_down, ((0, D_FF_PAD - D_FF), (0, 0))).astype(BF16)
    return wg, wu, cwg, cwu, cbg, cbu, wd


def _in_proj_weights(w):
    wt = w.T
    offs = _IN_OFFS
    seg = lambda a, b: wt[offs[a]:offs[b]]
    window = lambda i: wt[offs[i] - offs[i] % LANES:offs[i] - offs[i] % LANES + LANES]
    small = jnp.concatenate([window(2), window(10)], axis=0)
    cast = lambda a: a.astype(BF16)
    return dict(z=cast(seg(0, 1)), xbc=cast(seg(1, 2)), hgrn=cast(seg(3, 7)), fox_qk=cast(seg(7, 9)),
                fox_v=cast(seg(9, 10)), gates=cast(seg(11, 12)), small=cast(small))


def kernel(x, norm_mix_w, w_in, ssd_conv_w, ssd_conv_b, ssd_dt_bias, ssd_a_log, ssd_d, ssd_norm_w, hgrn_lb, hgrn_norm_w, fox_f_bias, w_branch_ssd, w_branch_hgrn, w_branch_fox, w_out, norm_ffn_w, ffn_w_up, ffn_conv_w, ffn_conv_b, ffn_w_down, final_norm_w):
    bsz, t, d = x.shape
    depth = w_in.shape[0]
    lbs = jnp.cumsum(jax.nn.softmax(hgrn_lb.astype(F32), axis=0), axis=0)
    lbs = lbs - lbs[0]
    x = x.reshape(bsz * t, d)
    h = _rmsnorm(x, norm_mix_w[0], BF16)
    for l in range(depth):
        wi = _in_proj_weights(w_in[l])
        proj = lambda key, dtype, tn, tm=1024: _mm(h, wi[key], dtype, tn, "proj_" + key, b_is_nk=True, tm=tm)
        z = proj("z", F32, 1024)
        xbc = proj("xbc", F32, SSD_XBC)
        small = proj("small", F32, 2 * LANES)
        p_hgrn = proj("hgrn", F32, 1024, 2048)
        fox_qk = proj("fox_qk", BF16, 1024, 2048)
        fox_vt = _mm_t(wi["fox_v"], h, BF16, "proj_fox_vt")
        gates = proj("gates", F32, 1024, 2048)
        y_ssd = _ssd(z, xbc, small, ssd_conv_w[l], ssd_conv_b[l], ssd_dt_bias[l], ssd_a_log[l], ssd_d[l],
                     ssd_norm_w[l], bsz, t)
        y_hgrn = _hgrn(p_hgrn, lbs[l], hgrn_norm_w[l], bsz, t)
        y_fox = _fox(fox_qk, fox_vt, *_fox_prep(small, fox_f_bias[l], bsz, t), bsz, t)
        merged = _merge((y_ssd, y_hgrn, y_fox),
                        (w_branch_ssd[l].astype(BF16), w_branch_hgrn[l].astype(BF16), w_branch_fox[l].astype(BF16)),
                        gates)
        x, h = _out_proj(merged, w_out[l].astype(BF16), x, norm_ffn_w[l])
        ffn_w = _ffn_weights(ffn_w_up, l, ffn_conv_w[l], ffn_conv_b[l], ffn_w_down[l])
        if l + 1 < depth:
            x, h = _ffn(h, x, norm_mix_w[l + 1], *ffn_w, t, True, BF16)
        else:
            (out,) = _ffn(h, x, final_norm_w, *ffn_w, t, False, F32)
    return out.reshape(bsz, t, d)
```

```python
import functools

import jax
import jax.numpy as jnp
import numpy as np
from jax import lax
from jax.experimental import pallas as pl
from jax.experimental.pallas import tpu as pltpu

F32 = jnp.float32
BF16 = jnp.bfloat16
LOG2E = 1.4426950408889634

D_MODEL = 2048
BRANCH = 1024
SSD_HEADS = 16
SSD_HEAD_DIM = 64
SSD_GROUPS = 2
SSD_STATE = 128
SSD_CONV = 4
SSD_CHUNK = 128
SSD_XBC = BRANCH + 2 * SSD_GROUPS * SSD_STATE
HGRN_HEADS = 8
HGRN_DK = 128
HGRN_CHUNK = 64
HGRN_SUB = 8
FOX_HEADS = 16
FOX_HEAD_DIM = 64
D_FF = 5504
FFN_CONV = 3
NORM_EPS = 1e-6
IN_SIZES = (BRANCH, SSD_XBC, SSD_HEADS, BRANCH, BRANCH, BRANCH, BRANCH, BRANCH, BRANCH, BRANCH,
            FOX_HEADS, 3 * D_MODEL)

LANES = 128
FF_TILE = 512
FF_SUB = 256
D_FF_PAD = 5632
VMEM_LIMIT = 56 * 1024 * 1024

_IN_OFFS = [sum(IN_SIZES[:i]) for i in range(len(IN_SIZES) + 1)]
DT_LANE0 = _IN_OFFS[2] % LANES
FOX_LANE0 = _IN_OFFS[10] % LANES
assert DT_LANE0 == 0 and FOX_LANE0 + FOX_HEADS <= LANES


def _params(*sem):
    return pltpu.CompilerParams(dimension_semantics=sem, vmem_limit_bytes=VMEM_LIMIT)


def _silu(x):
    return x * jax.nn.sigmoid(x)


def _log_sigmoid(x):
    return jnp.minimum(x, 0.0) - jnp.log(1.0 + jnp.exp(-jnp.abs(x)))


def _dot(a, b, **kw):
    return jnp.dot(a, b, preferred_element_type=F32, **kw)


def _dot_nt(a, b):
    return lax.dot_general(a, b, (((1,), (1,)), ((), ())), preferred_element_type=F32)


def _dot_tn(a, b):
    return lax.dot_general(a, b, (((0,), (0,)), ((), ())), preferred_element_type=F32)


def _tril(n):
    r = lax.broadcasted_iota(jnp.int32, (n, n), 0)
    c = lax.broadcasted_iota(jnp.int32, (n, n), 1)
    return (r >= c).astype(BF16)


def _split3(x):
    hi = x.astype(BF16)
    rest = x - hi.astype(F32)
    mid = rest.astype(BF16)
    lo = (rest - mid.astype(F32)).astype(BF16)
    return hi, mid, lo


def _select_dot(sel, x):
    return _dot(jnp.concatenate([sel, sel, sel], axis=1), jnp.concatenate(_split3(x), axis=0))


def _dot_select(x, sel):
    return _dot(jnp.concatenate(_split3(x), axis=1), jnp.concatenate([sel, sel, sel], axis=0))


def _rmsnorm_body(x_ref, w_ref, o_ref):
    x = x_ref[...]
    ms = jnp.mean(x * x, axis=-1, keepdims=True)
    o_ref[...] = (x * lax.rsqrt(ms + NORM_EPS) * w_ref[...]).astype(o_ref.dtype)


def _rmsnorm(x, w, out_dtype):
    m, d = x.shape
    tm = min(512, m)
    return pl.pallas_call(
        _rmsnorm_body,
        grid=(m // tm,),
        in_specs=[pl.BlockSpec((tm, d), lambda i: (i, 0)), pl.BlockSpec((1, d), lambda i: (0, 0))],
        out_specs=pl.BlockSpec((tm, d), lambda i: (i, 0)),
        out_shape=jax.ShapeDtypeStruct((m, d), out_dtype),
        compiler_params=_params("parallel"),
        name="rmsnorm",
    )(x, w.reshape(1, d))


def _mm_body(a_ref, b_ref, o_ref):
    o_ref[...] = _dot(a_ref[...], b_ref[...]).astype(o_ref.dtype)


def _mm_nt_body(a_ref, b_ref, o_ref):
    o_ref[...] = _dot_nt(a_ref[...], b_ref[...]).astype(o_ref.dtype)


def _out_proj_body(a_ref, b_ref, r_ref, nw_ref, o_ref, h_ref):
    x = r_ref[...] + _dot(a_ref[...], b_ref[...])
    o_ref[...] = x
    ms = jnp.mean(x * x, axis=-1, keepdims=True)
    h_ref[...] = (x * lax.rsqrt(ms + NORM_EPS) * nw_ref[...]).astype(h_ref.dtype)


def _out_proj(a, w, x, norm_w):
    m, k = a.shape
    d = w.shape[1]
    tm = min(512, m)
    row = lambda i: (i, 0)
    fixed = lambda i: (0, 0)
    return pl.pallas_call(
        _out_proj_body,
        grid=(m // tm,),
        in_specs=[pl.BlockSpec((tm, k), row), pl.BlockSpec((k, d), fixed, pipeline_mode=pl.Buffered(1)),
                  pl.BlockSpec((tm, d), row), pl.BlockSpec((1, d), fixed)],
        out_specs=[pl.BlockSpec((tm, d), row), pl.BlockSpec((tm, d), row)],
        out_shape=[jax.ShapeDtypeStruct((m, d), F32), jax.ShapeDtypeStruct((m, d), BF16)],
        compiler_params=_params("parallel"),
        name="out_proj",
    )(a, w, x, norm_w.reshape(1, d))


def _mm_t_body(w_ref, a_ref, o_ref):
    o_ref[...] = _dot_nt(w_ref[...], a_ref[...]).astype(o_ref.dtype)


def _mm_t(w, a, out_dtype, name, tm=1024):
    n, k = w.shape
    m = a.shape[0]
    tm = min(tm, m)
    return pl.pallas_call(
        _mm_t_body,
        grid=(m // tm,),
        in_specs=[pl.BlockSpec((n, k), lambda i: (0, 0), pipeline_mode=pl.Buffered(1)),
                  pl.BlockSpec((tm, k), lambda i: (i, 0))],
        out_specs=pl.BlockSpec((n, tm), lambda i: (0, i)),
        out_shape=jax.ShapeDtypeStruct((n, m), out_dtype),
        compiler_params=_params("parallel"),
        name=name,
    )(w, a)


def _mm(a, b, out_dtype, tn, name, b_is_nk=False, tm=1024):
    m, k = a.shape
    n = b.shape[0] if b_is_nk else b.shape[1]
    tm = min(tm, m)
    mode = dict(pipeline_mode=pl.Buffered(1)) if n == tn else {}
    b_spec = (pl.BlockSpec((tn, k), lambda i, j: (j, 0), **mode) if b_is_nk
              else pl.BlockSpec((k, tn), lambda i, j: (0, j), **mode))
    in_specs = [pl.BlockSpec((tm, k), lambda i, j: (i, 0)), b_spec]
    args = [a, b]
    body = _mm_nt_body if b_is_nk else _mm_body
    return pl.pallas_call(
        body,
        grid=(m // tm, n // tn),
        in_specs=in_specs,
        out_specs=pl.BlockSpec((tm, tn), lambda i, j: (i, j)),
        out_shape=jax.ShapeDtypeStruct((m, n), out_dtype),
        compiler_params=_params("parallel", "parallel"),
        name=name,
    )(*args)


def _ssd_body(z_ref, xbc_ref, sm_ref, cw_ref, cb_ref, dtb_ref, alog_ref, d_ref, nw_ref, o_ref,
              ext_ref, s_ref, *, tt):
    @pl.when(pl.program_id(1) == 0)
    def _():
        ext_ref[:8, :] = jnp.zeros((8, SSD_XBC), F32)
        s_ref[...] = jnp.zeros_like(s_ref)

    ext_ref[8:, :] = xbc_ref[...]
    cw = cw_ref[...]
    conv = cb_ref[...] + cw[SSD_CONV - 1:SSD_CONV] * xbc_ref[...]
    for k in range(SSD_CONV - 1):
        back = SSD_CONV - 1 - k
        conv = conv + cw[k:k + 1] * ext_ref[8 - back:8 - back + tt, :]
    ext_ref[:8, :] = xbc_ref[tt - 8:, :]
    act = _silu(conv)
    xs_all = act[:, :BRANCH]
    dt_all = jax.nn.softplus(sm_ref[...] + dtb_ref[...])
    adt_all = dt_all * (-LOG2E * jnp.exp(alog_ref[...]))

    ln = SSD_CHUNK
    tril = _tril(ln)
    row = lax.broadcasted_iota(jnp.int32, (ln, ln), 0)
    col = lax.broadcasted_iota(jnp.int32, (ln, ln), 1)
    causal = row >= col
    lane = lax.broadcasted_iota(jnp.int32, (ln, LANES), 1)
    eh = lax.broadcasted_iota(jnp.int32, (LANES, BRANCH), 0)
    ec = lax.broadcasted_iota(jnp.int32, (LANES, BRANCH), 1)
    expand = (ec // SSD_HEAD_DIM == eh).astype(BF16)
    gw = BRANCH // SSD_GROUPS

    for c in range(tt // ln):
        rs = slice(c * ln, (c + 1) * ln)
        xs = xs_all[rs]
        cs = _select_dot(tril, adt_all[rs])
        cs_t = cs.T
        cs_x = _dot_select(cs, expand)
        dt_x = _dot_select(dt_all[rs], expand)
        xdt = xs * dt_x
        last_x = cs_x[ln - 1:ln]
        xdt_st = (xdt * jnp.exp2(last_x - cs_x)).astype(BF16)
        out_decay = jnp.exp2(cs_x)
        chunk_decay = jnp.exp2(last_x)
        xdt_b = xdt.astype(BF16)
        ys = []
        for g in range(SSD_GROUPS):
            b_g = act[rs, BRANCH + g * SSD_STATE:BRANCH + (g + 1) * SSD_STATE].astype(BF16)
            c_g = act[rs, BRANCH + (SSD_GROUPS + g) * SSD_STATE:
                      BRANCH + (SSD_GROUPS + g + 1) * SSD_STATE].astype(BF16)
            gs = slice(g * gw, (g + 1) * gw)
            cb = _dot_nt(c_g, b_g)
            state = s_ref[g]
            y_off = _dot(c_g, state.astype(BF16)) * out_decay[:, gs]
            s_ref[g] = state * chunk_decay[:, gs] + _dot_tn(b_g, xdt_st[:, gs])
            heads_per_group = SSD_HEADS // SSD_GROUPS
            pieces = []
            for p in range(heads_per_group // 2):
                h0 = g * heads_per_group + 2 * p
                xp = xdt_b[:, h0 * SSD_HEAD_DIM:(h0 + 2) * SSD_HEAD_DIM]
                res = []
                for h in (h0, h0 + 1):
                    seg = jnp.where(causal, cs[:, h:h + 1] - cs_t[h:h + 1, :], -jnp.inf)
                    res.append(_dot((cb * jnp.exp2(seg)).astype(BF16), xp))
                pieces.append(jnp.where(lane < SSD_HEAD_DIM, res[0], res[1]))
            ys.append(jnp.concatenate(pieces, axis=1) + y_off)
        y = jnp.concatenate(ys, axis=1) + xs * d_ref[...]
        y = y * _silu(z_ref[rs, :])
        normed = []
        for g in range(SSD_GROUPS):
            yg = y[:, g * gw:(g + 1) * gw]
            ms = jnp.mean(yg * yg, axis=-1, keepdims=True)
            normed.append(yg * lax.rsqrt(ms + NORM_EPS))
        o_ref[rs, :] = (jnp.concatenate(normed, axis=1) * nw_ref[...]).astype(o_ref.dtype)


def _ssd(z, xbc, small, conv_w, conv_b, dt_bias, a_log, d_skip, norm_w, bsz, t):
    tt = min(512, t)
    nt = t // tt
    row = lambda b, i: (b * nt + i, 0)
    fixed = lambda b, i: (0, 0)
    pad = lambda v: jnp.pad(v, (0, LANES - SSD_HEADS)).reshape(1, LANES)
    return pl.pallas_call(
        functools.partial(_ssd_body, tt=tt),
        grid=(bsz, nt),
        in_specs=[
            pl.BlockSpec((tt, BRANCH), row),
            pl.BlockSpec((tt, SSD_XBC), row),
            pl.BlockSpec((tt, LANES), row),
            pl.BlockSpec((SSD_CONV, SSD_XBC), fixed),
            pl.BlockSpec((1, SSD_XBC), fixed),
            pl.BlockSpec((1, LANES), fixed),
            pl.BlockSpec((1, LANES), fixed),
            pl.BlockSpec((1, BRANCH), fixed),
            pl.BlockSpec((1, BRANCH), fixed),
        ],
        out_specs=pl.BlockSpec((tt, BRANCH), row),
        out_shape=jax.ShapeDtypeStruct((bsz * t, BRANCH), BF16),
        scratch_shapes=[pltpu.VMEM((8 + tt, SSD_XBC), F32),
                        pltpu.VMEM((SSD_GROUPS, SSD_STATE, BRANCH // SSD_GROUPS), F32)],
        compiler_params=_params("parallel", "arbitrary"),
        name="ssd",
    )(z, xbc, small, conv_w, conv_b.reshape(1, -1), pad(dt_bias), pad(a_log),
      jnp.repeat(d_skip, SSD_HEAD_DIM).reshape(1, -1), norm_w.reshape(1, -1))


def _hgrn_body(x_ref, loglb_ref, log1m_ref, onem_ref, nw_ref, o_ref, st_ref, res_ref, oi_ref, *, tt):
    @pl.when(pl.program_id(1) == 0)
    def _():
        st_ref[...] = jnp.zeros_like(st_ref)

    cl, sub = HGRN_CHUNK, HGRN_SUB
    nsub = cl // sub
    tril = _tril(cl)
    t_row = lax.broadcasted_iota(jnp.int32, (sub, cl), 0)
    lane = lax.broadcasted_iota(jnp.int32, (sub, cl), 1)

    def score_phase(sq, r0):
        hq = x_ref[sq, pl.ds(r0, cl), 0 * BRANCH:1 * BRANCH]
        hf = x_ref[sq, pl.ds(r0, cl), 1 * BRANCH:2 * BRANCH]
        hi = x_ref[sq, pl.ds(r0, cl), 2 * BRANCH:3 * BRANCH]
        e = jnp.exp(-jnp.abs(hf))
        d = 1.0 + e
        kk = onem_ref[...] * (jnp.where(hf >= 0.0, e, 1.0) / d)
        la = loglb_ref[...]
        lb2 = log1m_ref[...] + (jnp.minimum(hf, 0.0) - jnp.log(d))
        log_f = jnp.maximum(la, lb2) + jnp.log(1.0 + jnp.exp(-jnp.abs(la - lb2)))
        q = _silu(hq)
        b = _select_dot(tril, log_f * LOG2E)
        b_last = b[cl - 1:cl]
        q_in = (q * jnp.exp2(b)).astype(BF16)
        k_st = (kk * jnp.exp2(b_last - b)).astype(BF16)
        dec = jnp.exp2(b_last)
        v_b = hi.astype(BF16)
        for h in range(HGRN_HEADS):
            sl = slice(h * HGRN_DK, (h + 1) * HGRN_DK)
            bh, qh, kh = b[:, sl], q[:, sl], kk[:, sl]
            st = st_ref[sq, h]
            oi_ref[sq, h] = _dot_nt(q_in[:, sl], st.astype(BF16))
            st_ref[sq, h] = st * dec[:, sl] + _dot_tn(v_b[:, sl], k_st[:, sl])
            for i in range(nsub):
                i0 = i * sub
                bi, qi = bh[i0:i0 + sub], qh[i0:i0 + sub]
                lhs, rhs = [], []
                if i > 0:
                    beta = bh[i0 - 1:i0]
                    lhs.append(qi * jnp.exp2(bi - beta))
                    rhs.append(kh[:i0] * jnp.exp2(beta - bh[:i0]))
                for s in range(sub):
                    lhs.append(qi * jnp.exp2(bi - bi[s:s + 1]))
                rhs.append(kh[i0:i0 + sub])
                if i0 + sub < cl:
                    rhs.append(jnp.zeros((cl - i0 - sub, HGRN_DK), F32))
                res = _dot_nt(jnp.concatenate(lhs, axis=0).astype(BF16),
                              jnp.concatenate(rhs, axis=0).astype(BF16))
                res_ref[sq, h, i, (sub if i == 0 else 0):, :] = res

    def assemble_phase(sq, r0):
        hi = x_ref[sq, pl.ds(r0, cl), 2 * BRANCH:3 * BRANCH]
        hg = x_ref[sq, pl.ds(r0, cl), 3 * BRANCH:4 * BRANCH]
        v_b = hi.astype(BF16)
        outs = []
        for h in range(HGRN_HEADS):
            sl = slice(h * HGRN_DK, (h + 1) * HGRN_DK)
            a_rows = []
            for i in range(nsub):
                i0 = i * sub
                a_i = jnp.where(lane < i0, res_ref[sq, h, i, :sub, :], 0.0) if i > 0 else jnp.zeros((sub, cl), F32)
                for s in range(sub):
                    slab = res_ref[sq, h, i, (s + 1) * sub:(s + 2) * sub, :]
                    a_i = jnp.where((lane == i0 + s) & (t_row >= s), slab, a_i)
                a_rows.append(a_i)
            o_h = oi_ref[sq, h] + _dot(jnp.concatenate(a_rows, axis=0).astype(BF16), v_b[:, sl])
            ms = jnp.mean(o_h * o_h, axis=-1, keepdims=True)
            outs.append(o_h * lax.rsqrt(ms + NORM_EPS))
        o = jnp.concatenate(outs, axis=1) * nw_ref[...]
        o_ref[sq, pl.ds(r0, cl), :] = (o * _silu(hg)).astype(o_ref.dtype)

    def chunk(c, carry):
        r0 = pl.multiple_of(c * cl, cl)
        for sq in range(x_ref.shape[0]):
            score_phase(sq, r0)
        for sq in range(x_ref.shape[0]):
            assemble_phase(sq, r0)
        return carry

    lax.fori_loop(0, tt // cl, chunk, 0)


def _hgrn(proj, lb, norm_w, bsz, t):
    tt = min(256, t)
    nt = t // tt
    nseq = 2 if bsz % 2 == 0 else 1
    blk = lambda b, i: (b, i, 0)
    fixed = lambda b, i: (0, 0)
    lb = lb.reshape(1, BRANCH)
    out = pl.pallas_call(
        functools.partial(_hgrn_body, tt=tt),
        grid=(bsz // nseq, nt),
        in_specs=[pl.BlockSpec((nseq, tt, 4 * BRANCH), blk)] + [pl.BlockSpec((1, BRANCH), fixed)] * 4,
        out_specs=pl.BlockSpec((nseq, tt, BRANCH), blk),
        out_shape=jax.ShapeDtypeStruct((bsz, t, BRANCH), BF16),
        scratch_shapes=[pltpu.VMEM((nseq, HGRN_HEADS, HGRN_DK, HGRN_DK), F32),
                        pltpu.VMEM((nseq, HGRN_HEADS, HGRN_CHUNK // HGRN_SUB, HGRN_SUB * (HGRN_SUB + 1), HGRN_CHUNK), F32),
                        pltpu.VMEM((nseq, HGRN_HEADS, HGRN_CHUNK, HGRN_DK), F32)],
        compiler_params=_params("parallel", "arbitrary"),
        name="hgrn",
    )(proj.reshape(bsz, t, 4 * BRANCH), jnp.log(lb), jnp.log1p(-lb), 1.0 - lb, norm_w.reshape(1, BRANCH))
    return out.reshape(bsz * t, BRANCH)


FOX_BIAS_STRIDE = 8


def _fox_bias_tables():
    sel_q = np.zeros((3 * LANES, BRANCH), np.float32)
    sel_k = np.zeros((3 * LANES, BRANCH), np.float32)
    one_q = np.zeros((1, BRANCH), np.float32)
    one_k = np.zeros((1, BRANCH), np.float32)
    for h in range(FOX_HEADS):
        base = (h // 2) * LANES + FOX_BIAS_STRIDE * (h % 2)
        for j in range(3):
            sel_q[j * LANES + FOX_LANE0 + h, base + j] = 1.0
            sel_k[j * LANES + FOX_LANE0 + h, base + 3 + j] = -1.0
            one_q[0, base + 3 + j] = 1.0
            one_k[0, base + j] = 1.0
    return jnp.asarray(sel_q, BF16), jnp.asarray(sel_k, BF16), jnp.asarray(one_q), jnp.asarray(one_k)


def _fox_prep_body(sm_ref, bias_ref, selq_ref, selk_ref, oneq_ref, onek_ref, qb_ref, kb_ref, pieces_ref, *, t):
    ln = LANES
    tril = _tril(ln)
    carry = jnp.zeros((1, LANES), F32)
    for c in range(t // ln):
        rs = slice(c * ln, (c + 1) * ln)
        lf = _log_sigmoid(sm_ref[rs, :] + bias_ref[...])
        cs = _select_dot(tril, lf) + carry
        carry = cs[ln - 1:ln]
        hi = cs.astype(BF16)
        rest = cs - hi.astype(F32)
        mid = rest.astype(BF16)
        lo = (rest - mid.astype(F32)).astype(BF16)
        pieces_ref[rs, :] = jnp.concatenate([hi, mid, lo], axis=1)
    pieces = pieces_ref[...]
    qb_ref[...] = (_dot(pieces, selq_ref[...]) + oneq_ref[...]).astype(BF16)
    kb_ref[...] = (_dot(pieces, selk_ref[...]) + onek_ref[...]).astype(BF16)


def _fox_prep(small, f_bias, bsz, t):
    bias = jnp.pad(f_bias, (FOX_LANE0, LANES - FOX_LANE0 - FOX_HEADS)).reshape(1, LANES)
    fixed = lambda b: (0, 0)
    out = jax.ShapeDtypeStruct((bsz * t, BRANCH), BF16)
    return pl.pallas_call(
        functools.partial(_fox_prep_body, t=t),
        grid=(bsz,),
        in_specs=[pl.BlockSpec((t, LANES), lambda b: (b, 1)), pl.BlockSpec((1, LANES), fixed),
                  pl.BlockSpec((3 * LANES, BRANCH), fixed), pl.BlockSpec((3 * LANES, BRANCH), fixed),
                  pl.BlockSpec((1, BRANCH), fixed), pl.BlockSpec((1, BRANCH), fixed)],
        out_specs=[pl.BlockSpec((t, BRANCH), lambda b: (b, 0))] * 2,
        out_shape=[out, out],
        scratch_shapes=[pltpu.VMEM((t, 3 * LANES), BF16)],
        compiler_params=_params("parallel"),
        name="fox_prep",
    )(small, bias, *_fox_bias_tables())


FOX_ONES_ROWS = 16


FOX_PAIRS_PER_STEP = 4


def _fox_body(q_ref, k_ref, vt_in_ref, qb_ref, kb_ref, o_ref, *scratch, tq, t):
    qi = pl.program_id(2)
    hd = FOX_HEAD_DIM
    lane = lax.broadcasted_iota(jnp.int32, (tq, LANES), 1)
    sub = lax.broadcasted_iota(jnp.int32, (LANES, tq), 0)
    key = lax.broadcasted_iota(jnp.int32, (tq, 2 * tq), 0)
    qry = lax.broadcasted_iota(jnp.int32, (tq, 2 * tq), 1) % tq
    pair_scratch = [scratch[4 * r:4 * r + 4] for r in range(FOX_PAIRS_PER_STEP)]

    @pl.when(qi == 0)
    def _():
        for r, (kaug_ref, vt_ref, _, _) in enumerate(pair_scratch):
            ls = slice(r * LANES, (r + 1) * LANES)
            kaug_ref[:, :LANES] = k_ref[:, ls]
            kaug_ref[:, LANES:] = kb_ref[:, ls]
            vt_ref[:LANES, :] = vt_in_ref[ls, :]
            vt_ref[LANES:, :] = jnp.ones((FOX_ONES_ROWS, t), BF16)

    q_augs = []
    for r in range(FOX_PAIRS_PER_STEP):
        ls = slice(r * LANES, (r + 1) * LANES)
        q = q_ref[:, ls] * (hd ** -0.5)
        qb = qb_ref[:, ls]
        rows = []
        for h in range(2):
            mine = (lane < hd) if h == 0 else (lane >= hd)
            mine_b = (lane < FOX_BIAS_STRIDE) if h == 0 else (lane >= FOX_BIAS_STRIDE)
            rows.append(jnp.concatenate([jnp.where(mine, q, jnp.zeros_like(q)),
                                         jnp.where(mine_b, qb, jnp.zeros_like(qb))], axis=1))
        q_augs.append(jnp.concatenate(rows, axis=0))

    for c in range(t // tq):
        @pl.when(qi == c)
        def _(c=c):
            nk = (c + 1) * tq
            for r, (kaug_ref, vt_ref, s_ref, p_ref) in enumerate(pair_scratch):
                s_ref[:nk, :] = _dot_nt(kaug_ref[:nk, :], q_augs[r])
            for r, (kaug_ref, vt_ref, s_ref, p_ref) in enumerate(pair_scratch):
                diag = jnp.where(key <= qry, s_ref[c * tq:nk, :], -jnp.inf)
                m = jnp.max(diag, axis=0, keepdims=True)
                if c > 0:
                    m = jnp.maximum(m, jnp.max(s_ref[:c * tq, :], axis=0, keepdims=True))
                    p_ref[:c * tq, :] = jnp.exp(s_ref[:c * tq, :] - m).astype(BF16)
                p_ref[c * tq:nk, :] = jnp.exp(diag - m).astype(BF16)
            for r, (kaug_ref, vt_ref, s_ref, p_ref) in enumerate(pair_scratch):
                acc = _dot(vt_ref[:, :nk], p_ref[:nk, :])
                o0 = acc[:LANES, :tq] / acc[LANES:LANES + 1, :tq]
                o1 = acc[:LANES, tq:] / acc[LANES:LANES + 1, tq:]
                o_ref[:, r * LANES:(r + 1) * LANES] = jnp.where(sub < hd, o0, o1).T.astype(o_ref.dtype)


def _fox(qk, vt, qbias, kbias, bsz, t):
    tq = min(256, t)
    nq = t // tq
    w = FOX_PAIRS_PER_STEP * LANES
    groups = BRANCH // w
    per_pair = [pltpu.VMEM((t, 2 * LANES), BF16), pltpu.VMEM((LANES + FOX_ONES_ROWS, t), BF16),
                pltpu.VMEM((t, 2 * tq), F32), pltpu.VMEM((t, 2 * tq), BF16)]
    return pl.pallas_call(
        functools.partial(_fox_body, tq=tq, t=t),
        grid=(bsz, groups, nq),
        in_specs=[
            pl.BlockSpec((tq, w), lambda b, g, i: (b * nq + i, g)),
            pl.BlockSpec((t, w), lambda b, g, i: (b, groups + g)),
            pl.BlockSpec((w, t), lambda b, g, i: (g, b)),
            pl.BlockSpec((tq, w), lambda b, g, i: (b * nq + i, g)),
            pl.BlockSpec((t, w), lambda b, g, i: (b, g)),
        ],
        out_specs=pl.BlockSpec((tq, w), lambda b, g, i: (b * nq + i, g)),
        out_shape=jax.ShapeDtypeStruct((bsz * t, BRANCH), BF16),
        scratch_shapes=per_pair * FOX_PAIRS_PER_STEP,
        compiler_params=_params("parallel", "parallel", "arbitrary"),
        name="fox",
    )(qk, qk, vt, qbias, kbias)


def _merge_body(y0_ref, y1_ref, y2_ref, w0_ref, w1_ref, w2_ref, g0_ref, g1_ref, g2_ref, o_ref):
    acc = jax.nn.sigmoid(g0_ref[...]) * _dot(y0_ref[...], w0_ref[...])
    acc = acc + jax.nn.sigmoid(g1_ref[...]) * _dot(y1_ref[...], w1_ref[...])
    acc = acc + jax.nn.sigmoid(g2_ref[...]) * _dot(y2_ref[...], w2_ref[...])
    o_ref[...] = acc.astype(o_ref.dtype)


def _merge(ys, ws, gates):
    m = gates.shape[0]
    tm = min(512, m)
    y_spec = pl.BlockSpec((tm, BRANCH), lambda i: (i, 0))
    w_spec = pl.BlockSpec((BRANCH, D_MODEL), lambda i: (0, 0), pipeline_mode=pl.Buffered(1))
    g_specs = [pl.BlockSpec((tm, D_MODEL), functools.partial(lambda i, r: (i, r), r=r)) for r in range(3)]
    return pl.pallas_call(
        _merge_body,
        grid=(m // tm,),
        in_specs=[y_spec] * 3 + [w_spec] * 3 + g_specs,
        out_specs=pl.BlockSpec((tm, D_MODEL), lambda i: (i, 0)),
        out_shape=jax.ShapeDtypeStruct((m, D_MODEL), BF16),
        compiler_params=_params("parallel"),
        name="merge",
    )(*ys, *ws, gates, gates, gates)


def _ffn_body(h_ref, x_ref, nw_ref, wg_ref, wu_ref, cwg_ref, cwu_ref, cbg_ref, cbu_ref, wd_ref,
              *rest, tm, tiles_per_seq, emit_x):
    if emit_x:
        o_ref, n_ref, acc_ref, ug_ref, uu_ref, carry_ref = rest
    else:
        n_ref, acc_ref, ug_ref, uu_ref, carry_ref = rest
    i = pl.program_id(0)
    j = pl.program_id(1)
    pad = 8
    first = (i % tiles_per_seq) == 0

    @pl.when(j == 0)
    def _():
        acc_ref[...] = x_ref[...]

    @pl.when(first)
    def _():
        ug_ref[:pad, :] = jnp.zeros((pad, FF_TILE), F32)
        uu_ref[:pad, :] = jnp.zeros((pad, FF_TILE), F32)

    @pl.when(jnp.logical_not(first))
    def _():
        ug_ref[:pad, :] = carry_ref[j, 0]
        uu_ref[:pad, :] = carry_ref[j, 1]

    hh = h_ref[...]
    ug_ref[pad:, :] = _dot(hh, wg_ref[...])
    uu_ref[pad:, :] = _dot(hh, wu_ref[...])
    carry_ref[j, 0] = ug_ref[tm:, :]
    carry_ref[j, 1] = uu_ref[tm:, :]

    def conv(u_ref, cw_ref, cb_ref, cs):
        cw = cw_ref[:, cs]
        out = cb_ref[:, cs] + cw[FFN_CONV - 1:FFN_CONV] * u_ref[pad:, cs]
        for k in range(FFN_CONV - 1):
            back = FFN_CONV - 1 - k
            out = out + cw[k:k + 1] * u_ref[pad - back:pad - back + tm, cs]
        return out

    down = None
    for s0 in range(0, FF_TILE, FF_SUB):
        cs = slice(s0, s0 + FF_SUB)
        act = (_silu(conv(ug_ref, cwg_ref, cbg_ref, cs)) * conv(uu_ref, cwu_ref, cbu_ref, cs)).astype(BF16)
        part = _dot(act, wd_ref[cs, :])
        down = part if down is None else down + part
    acc_ref[...] += down

    @pl.when(j == pl.num_programs(1) - 1)
    def _():
        x_new = acc_ref[...]
        if emit_x:
            o_ref[...] = x_new
        ms = jnp.mean(x_new * x_new, axis=-1, keepdims=True)
        n_ref[...] = (x_new * lax.rsqrt(ms + NORM_EPS) * nw_ref[...]).astype(n_ref.dtype)


def _ffn(h, x, next_norm_w, wg, wu, cwg, cwu, cbg, cbu, wd, t, emit_x, norm_dtype):
    m = x.shape[0]
    tm = min(512, t)
    nf = D_FF_PAD // FF_TILE
    col = lambda i, j: (0, j)
    row = lambda i, j: (i, 0)
    out_specs = [pl.BlockSpec((tm, D_MODEL), row)]
    out_shape = [jax.ShapeDtypeStruct((m, D_MODEL), norm_dtype)]
    if emit_x:
        out_specs = [pl.BlockSpec((tm, D_MODEL), row)] + out_specs
        out_shape = [jax.ShapeDtypeStruct((m, D_MODEL), F32)] + out_shape
    return pl.pallas_call(
        functools.partial(_ffn_body, tm=tm, tiles_per_seq=t // tm, emit_x=emit_x),
        grid=(m // tm, nf),
        in_specs=[
            pl.BlockSpec((tm, D_MODEL), row),
            pl.BlockSpec((tm, D_MODEL), row),
            pl.BlockSpec((1, D_MODEL), lambda i, j: (0, 0)),
            pl.BlockSpec((D_MODEL, FF_TILE), col),
            pl.BlockSpec((D_MODEL, FF_TILE), col),
            pl.BlockSpec((FFN_CONV, FF_TILE), col),
            pl.BlockSpec((FFN_CONV, FF_TILE), col),
            pl.BlockSpec((1, FF_TILE), col),
            pl.BlockSpec((1, FF_TILE), col),
            pl.BlockSpec((FF_TILE, D_MODEL), lambda i, j: (j, 0)),
        ],
        out_specs=out_specs,
        out_shape=out_shape,
        scratch_shapes=[pltpu.VMEM((tm, D_MODEL), F32),
                        pltpu.VMEM((8 + tm, FF_TILE), F32), pltpu.VMEM((8 + tm, FF_TILE), F32),
                        pltpu.VMEM((nf, 2, 8, FF_TILE), F32)],
        compiler_params=_params("arbitrary", "arbitrary"),
        name="ffn",
    )(h, x, next_norm_w.reshape(1, D_MODEL), wg, wu, cwg, cwu, cbg, cbu, wd)


def _split_up_body(w_ref, g_ref, u_ref):
    zeros = jnp.zeros((g_ref.shape[0], D_FF_PAD - D_FF), BF16)
    g_ref[:, :D_FF] = w_ref[:, :D_FF].astype(BF16)
    g_ref[:, D_FF:] = zeros
    u_ref[:, :D_FF] = w_ref[:, D_FF:].astype(BF16)
    u_ref[:, D_FF:] = zeros


def _split_up_weights(w_up_all, l):
    tr = 256
    out = jax.ShapeDtypeStruct((D_MODEL, D_FF_PAD), BF16)
    return pl.pallas_call(
        _split_up_body,
        grid=(D_MODEL // tr,),
        in_specs=[pl.BlockSpec((None, tr, 2 * D_FF), lambda i: (l, i, 0))],
        out_specs=[pl.BlockSpec((tr, D_FF_PAD), lambda i: (i, 0))] * 2,
        out_shape=[out, out],
        compiler_params=_params("parallel"),
        name="split_up_weights",
    )(w_up_all)


def _ffn_weights(w_up_all, l, conv_w, conv_b, w_down):
    padc = lambda a: jnp.pad(a, [(0, 0)] * (a.ndim - 1) + [(0, D_FF_PAD - D_FF)])
    halves = lambda a: (padc(a[..., :D_FF]), padc(a[..., D_FF:]))
    wg, wu = _split_up_weights(w_up_all, l)
    cwg, cwu = halves(conv_w)
    cbg, cbu = halves(conv_b.reshape(1, -1))
    wd = jnp.pad(w_down, ((0, D_FF_PAD - D_FF), (0, 0))).astype(BF16)
    return wg, wu, cwg, cwu, cbg, cbu, wd


def _in_proj_weights(w):
    wt = w.T
    offs = _IN_OFFS
    seg = lambda a, b: wt[offs[a]:offs[b]]
    window = lambda i: wt[offs[i] - offs[i] % LANES:offs[i] - offs[i] % LANES + LANES]
    small = jnp.concatenate([window(2), window(10)], axis=0)
    cast = lambda a: a.astype(BF16)
    return dict(z=cast(seg(0, 1)), xbc=cast(seg(1, 2)), hgrn=cast(seg(3, 7)), fox_qk=cast(seg(7, 9)),
                fox_v=cast(seg(9, 10)), gates=cast(seg(11, 12)), small=cast(small))


def kernel(x, norm_mix_w, w_in, ssd_conv_w, ssd_conv_b, ssd_dt_bias, ssd_a_log, ssd_d, ssd_norm_w, hgrn_lb, hgrn_norm_w, fox_f_bias, w_branch_ssd, w_branch_hgrn, w_branch_fox, w_out, norm_ffn_w, ffn_w_up, ffn_conv_w, ffn_conv_b, ffn_w_down, final_norm_w):
    bsz, t, d = x.shape
    depth = w_in.shape[0]
    lbs = jnp.cumsum(jax.nn.softmax(hgrn_lb.astype(F32), axis=0), axis=0)
    lbs = lbs - lbs[0]
    x = x.reshape(bsz * t, d)
    h = _rmsnorm(x, norm_mix_w[0], BF16)
    for l in range(depth):
        wi = _in_proj_weights(w_in[l])
        proj = lambda key, dtype, tn, tm=1024: _mm(h, wi[key], dtype, tn, "proj_" + key, b_is_nk=True, tm=tm)
        z = proj("z", F32, 1024)
        xbc = proj("xbc", F32, SSD_XBC)
        small = proj("small", F32, 2 * LANES)
        p_hgrn = proj("hgrn", F32, 1024, 2048)
        fox_qk = proj("fox_qk", BF16, 1024, 2048)
        fox_vt = _mm_t(wi["fox_v"], h, BF16, "proj_fox_vt")
        gates = proj("gates", F32, 1024, 2048)
        y_ssd = _ssd(z, xbc, small, ssd_conv_w[l], ssd_conv_b[l], ssd_dt_bias[l], ssd_a_log[l], ssd_d[l],
                     ssd_norm_w[l], bsz, t)
        y_hgrn = _hgrn(p_hgrn, lbs[l], hgrn_norm_w[l], bsz, t)
        y_fox = _fox(fox_qk, fox_vt, *_fox_prep(small, fox_f_bias[l], bsz, t), bsz, t)
        merged = _merge((y_ssd, y_hgrn, y_fox),
                        (w_branch_ssd[l].astype(BF16), w_branch_hgrn[l].astype(BF16), w_branch_fox[l].astype(BF16)),
                        gates)
        x, h = _out_proj(merged, w_out[l].astype(BF16), x, norm_ffn_w[l])
        ffn_w = _ffn_weights(ffn_w_up, l, ffn_conv_w[l], ffn_conv_b[l], ffn_w_down[l])
        if l + 1 < depth:
            x, h = _ffn(h, x, norm_mix_w[l + 1], *ffn_w, t, True, BF16)
        else:
            (out,) = _ffn(h, x, final_norm_w, *ffn_w, t, False, F32)
    return out.reshape(bsz, t, d)
```
